```python
import math
import jax, jax.numpy as jnp
from jax import lax
import numpy as np

D_MODEL = 2048
BATCH = 16
SEQ = 256
DEPTH = 1
DEC_BATCH = 8
DEC_SEQ = 2048
PAST_LEN = 256

GRID_W = 64
N_RET_HEADS = 8
RET_DK = D_MODEL // N_RET_HEADS
RET_DV = 2 * RET_DK
RET_CHUNK = 128
N_DIFF_HEADS = 8
DIFF_DH = D_MODEL // N_DIFF_HEADS // 2
ROPE_HALF = DIFF_DH // 2
ROPE_BASE = 10000.0
Q_BLOCK = 128
D_FF = 4 * D_MODEL
N_MOD = 6
RET_QK_W = N_RET_HEADS * RET_DK
RET_V_W = N_RET_HEADS * RET_DV
DIFF_QK_W = N_DIFF_HEADS * 2 * DIFF_DH
DIFF_V_W = N_DIFF_HEADS * 2 * DIFF_DH
GATE_W = 2 * D_MODEL
IN_SPLITS = (RET_QK_W, RET_QK_W, RET_V_W, RET_V_W, DIFF_QK_W, DIFF_QK_W, DIFF_V_W, GATE_W)
IN_W = sum(IN_SPLITS)
IN_OFFSETS = [int(o) for o in np.cumsum(IN_SPLITS)[:-1]]
EPS = 1e-6
SUBLN_EPS = 1e-5

kernel_name = 'hybrid_retention_diffattn_prefix_dit_step'

F32 = jnp.float32


def rmsnorm(x, g=None, eps=EPS):
    xf = x.astype(F32)
    y = xf * lax.rsqrt(jnp.mean(xf * xf, axis=-1, keepdims=True) + eps)
    if g is not None:
        y = y * g.astype(F32)
    return y.astype(x.dtype)


def adaln(cond, w_ada, b_ada):
    m = jax.nn.silu(cond) @ w_ada + b_ada
    return jnp.split(m, N_MOD, axis=-1)


def modulate(x, g, shift, scale):
    return rmsnorm(x, g) * (1.0 + scale) + shift


def project_in(h, w_in):
    B, T, _ = h.shape
    p = h @ w_in
    rq, rk, rv, rg, dq, dk, dv, gates = jnp.split(p, IN_OFFSETS, axis=-1)
    rq = rq.reshape(B, T, N_RET_HEADS, RET_DK).transpose(0, 2, 1, 3)
    rk = rk.reshape(B, T, N_RET_HEADS, RET_DK).transpose(0, 2, 1, 3) * (RET_DK ** -0.5)
    rv = rv.reshape(B, T, N_RET_HEADS, RET_DV).transpose(0, 2, 1, 3)
    dq = dq.reshape(B, T, N_DIFF_HEADS, 2, DIFF_DH)
    dk = dk.reshape(B, T, N_DIFF_HEADS, 2, DIFF_DH)
    dv = dv.reshape(B, T, N_DIFF_HEADS, 2 * DIFF_DH)
    gate_ret, gate_diff = jnp.split(gates, 2, axis=-1)
    return rq, rk, rv, rg, dq, dk, dv, gate_ret, gate_diff


def retention_chunked(q, k, v, log_gamma, s0):
    B, H, T, _ = q.shape
    n = T // RET_CHUNK
    idx = jnp.arange(RET_CHUNK, dtype=F32)
    lg = log_gamma[:, None]
    rel = idx[:, None] - idx[None, :]
    decay = jnp.where(rel >= 0, jnp.exp(jnp.maximum(rel, 0.0)[None] * lg[:, :, None]), 0.0)
    xi = jnp.exp((idx + 1.0)[None] * lg)
    zeta = jnp.exp((RET_CHUNK - 1.0 - idx)[None] * lg)
    g_chunk = jnp.exp(RET_CHUNK * log_gamma)

    def to_chunks(a):
        return jnp.moveaxis(a.astype(F32).reshape(B, H, n, RET_CHUNK, a.shape[-1]), 2, 0)

    def step(S, blk):
        qc, kc, vc = blk
        scores = jnp.einsum('bhid,bhjd->bhij', qc, kc) * decay[None]
        o = (jnp.einsum('bhij,bhjv->bhiv', scores, vc)
             + jnp.einsum('bhid,bhdv->bhiv', qc * xi[None, :, :, None], S))
        S = (g_chunk[None, :, None, None] * S
             + jnp.einsum('bhjd,bhjv->bhdv', kc * zeta[None, :, :, None], vc))
        return S, o

    s_fin, o = lax.scan(step, s0.astype(F32), (to_chunks(q), to_chunks(k), to_chunks(v)))
    o = jnp.moveaxis(o, 0, 2).reshape(B, H, T, v.shape[-1])
    return o, s_fin


def retention_bidir(q, k, v, lg_f, lg_b, s0_f, s0_b):
    o_f, s_f = retention_chunked(q, k, v, lg_f, s0_f)
    o_b, s_b = retention_chunked(jnp.flip(q, 2), jnp.flip(k, 2), jnp.flip(v, 2), lg_b, s0_b)
    return o_f + jnp.flip(o_b, 2), s_f, s_b


def retention_out(o, rg, w_ret_o):
    B, H, T, _ = o.shape
    o = rmsnorm(o)
    o = o.transpose(0, 2, 1, 3).reshape(B, T, RET_V_W).astype(rg.dtype)
    return (jax.nn.silu(rg) * o) @ w_ret_o


def diff_lambda(lq1, lk1, lq2, lk2, lam_init):
    return (jnp.exp(jnp.sum(lq1.astype(F32) * lk1.astype(F32)))
            - jnp.exp(jnp.sum(lq2.astype(F32) * lk2.astype(F32))) + lam_init)


def diff_attend(q, k, v, lam):
    s = jnp.einsum('bqhmd,bkhmd->bhmqk', q.astype(F32), k.astype(F32)) * (DIFF_DH ** -0.5)
    a = jax.nn.softmax(s, axis=-1)
    a = a[:, :, 0] - lam * a[:, :, 1]
    return jnp.einsum('bhqk,bkhe->bqhe', a, v.astype(F32))


def diff_attend_blocked(q, k, v, lam):
    B, T = q.shape[0], q.shape[1]
    nb = T // Q_BLOCK
    qb = q.reshape(B, nb, Q_BLOCK, N_DIFF_HEADS, 2, DIFF_DH).transpose(1, 0, 2, 3, 4, 5)
    ob = lax.map(lambda qq: diff_attend(qq, k, v, lam), qb)
    return ob.transpose(1, 0, 2, 3, 4).reshape(B, T, N_DIFF_HEADS, 2 * DIFF_DH)


def diff_out(o, g_subln, lam_init, w_diff_o, dtype):
    B, T = o.shape[0], o.shape[1]
    o = rmsnorm(o, g_subln, SUBLN_EPS) * (1.0 - lam_init)
    return o.reshape(B, T, DIFF_V_W).astype(dtype) @ w_diff_o


def axial_rope_angles(T):
    ROWS = T // GRID_W
    row = jnp.repeat(jnp.arange(ROWS, dtype=F32), GRID_W)
    col = jnp.tile(jnp.arange(GRID_W, dtype=F32), ROWS)
    inv = ROPE_BASE ** (-jnp.arange(0, ROPE_HALF, 2, dtype=F32) / ROPE_HALF)
    return row[:, None] * inv, col[:, None] * inv


def rotate_half_block(x, ang):
    cos = jnp.cos(ang)[:, None, None, :]
    sin = jnp.sin(ang)[:, None, None, :]
    x1, x2 = x[..., :ROPE_HALF // 2], x[..., ROPE_HALF // 2:]
    return jnp.concatenate([x1 * cos - x2 * sin, x2 * cos + x1 * sin], axis=-1)


def apply_axial_rope(x, ang_r, ang_c):
    xf = x.astype(F32)
    out = jnp.concatenate([rotate_half_block(xf[..., :ROPE_HALF], ang_r),
                           rotate_half_block(xf[..., ROPE_HALF:], ang_c)], axis=-1)
    return out.astype(x.dtype)


def merge_branches(ret_b, diff_b, gate_ret, gate_diff, w_out):
    return (jax.nn.sigmoid(gate_ret) * ret_b + jax.nn.sigmoid(gate_diff) * diff_b) @ w_out


def channel_mlp(h, w_up, w_down):
    return jnp.square(jax.nn.relu(h @ w_up)) @ w_down


def setup_inputs(seed: int = 0) -> dict:
    key = jax.random.key(seed)
    ks = jax.random.split(key, 32)
    nrm = jax.random.normal
    D = D_MODEL
    gamma_logit0 = jnp.log(2.0 ** (5.0 + jnp.arange(N_RET_HEADS, dtype=F32)) - 1.0)
    return {
        'x_prompt': nrm(ks[0], (BATCH, SEQ, D), F32),
        'x_sample': nrm(ks[1], (DEC_BATCH, DEC_SEQ, D), F32),
        'cache_k': nrm(ks[2], (DEC_BATCH, DEPTH, PAST_LEN, N_DIFF_HEADS, 2, DIFF_DH), F32),
        'cache_v': nrm(ks[3], (DEC_BATCH, DEPTH, PAST_LEN, N_DIFF_HEADS, 2 * DIFF_DH), F32),
        'state_ret_fwd': 0.5 * nrm(ks[4], (DEC_BATCH, DEPTH, N_RET_HEADS, RET_DK, RET_DV), F32),
        'state_ret_bwd': 0.5 * nrm(ks[5], (DEC_BATCH, DEPTH, N_RET_HEADS, RET_DK, RET_DV), F32),
        'c': nrm(ks[6], (DEC_BATCH, D), F32),
        'c_ctx': nrm(ks[7], (D,), F32),
        'w_ada': 0.2 * D ** -0.5 * nrm(ks[8], (DEPTH, D, N_MOD * D), F32),
        'b_ada': 0.01 * nrm(ks[9], (DEPTH, N_MOD * D), F32),
        'g_mix_pre': 1.0 + 0.01 * nrm(ks[10], (DEPTH, D), F32),
        'g_mix_post': 1.0 + 0.01 * nrm(ks[11], (DEPTH, D), F32),
        'g_mlp_pre': 1.0 + 0.01 * nrm(ks[12], (DEPTH, D), F32),
        'g_mlp_post': 1.0 + 0.01 * nrm(ks[13], (DEPTH, D), F32),
        'w_in': D ** -0.5 * nrm(ks[14], (DEPTH, D, IN_W), F32),
        'ret_gamma_logit_fwd': gamma_logit0 + 0.1 * nrm(ks[15], (DEPTH, N_RET_HEADS), F32),
        'ret_gamma_logit_bwd': gamma_logit0 + 0.1 * nrm(ks[16], (DEPTH, N_RET_HEADS), F32),
        'w_ret_o': RET_V_W ** -0.5 * nrm(ks[17], (DEPTH, RET_V_W, D), F32),
        'lambda_q1': 0.1 * nrm(ks[18], (DEPTH, DIFF_DH), F32),
        'lambda_k1': 0.1 * nrm(ks[19], (DEPTH, DIFF_DH), F32),
        'lambda_q2': 0.1 * nrm(ks[20], (DEPTH, DIFF_DH), F32),
        'lambda_k2': 0.1 * nrm(ks[21], (DEPTH, DIFF_DH), F32),
        'g_diff_subln': 1.0 + 0.01 * nrm(ks[22], (DEPTH, 2 * DIFF_DH), F32),
        'w_diff_o': DIFF_V_W ** -0.5 * nrm(ks[23], (DEPTH, DIFF_V_W, D), F32),
        'w_out': D ** -0.5 * nrm(ks[24], (DEPTH, D, D), F32),
        'w_mlp_up': D ** -0.5 * nrm(ks[25], (DEPTH, D, D_FF), F32),
        'w_mlp_down': D_FF ** -0.5 * nrm(ks[26], (DEPTH, D_FF, D), F32),
    }


def reference(x_prompt, x_sample, cache_k, cache_v, state_ret_fwd, state_ret_bwd, c, c_ctx,
              w_ada, b_ada, g_mix_pre, g_mix_post, g_mlp_pre, g_mlp_post, w_in,
              ret_gamma_logit_fwd, ret_gamma_logit_bwd, w_ret_o,
              lambda_q1, lambda_k1, lambda_q2, lambda_k2, g_diff_subln, w_diff_o, w_out,
              w_mlp_up, w_mlp_down):
    xp = x_prompt
    Bp = xp.shape[0]
    new_k, new_v, new_sf, new_sb = [], [], [], []
    for l in range(DEPTH):
        lam_init = 0.8 - 0.6 * math.exp(-0.3 * l)
        lam = diff_lambda(lambda_q1[l], lambda_k1[l], lambda_q2[l], lambda_k2[l], lam_init)
        lg_f = jax.nn.log_sigmoid(ret_gamma_logit_fwd[l].astype(F32))
        lg_b = jax.nn.log_sigmoid(ret_gamma_logit_bwd[l].astype(F32))
        sh1, sc1, gt1, sh2, sc2, gt2 = adaln(c_ctx, w_ada[l], b_ada[l])
        h = modulate(xp, g_mix_pre[l], sh1, sc1)
        rq, rk, rv, rg, dq, dk, dv, gr, gd = project_in(h, w_in[l])
        zero_state = jnp.zeros((Bp, N_RET_HEADS, RET_DK, RET_DV), F32)
        ro, s_f, s_b = retention_bidir(rq, rk, rv, lg_f, lg_b, zero_state, zero_state)
        ret_b = retention_out(ro, rg, w_ret_o[l])
        do = diff_attend(dq, dk, dv, lam)
        diff_b = diff_out(do, g_diff_subln[l], lam_init, w_diff_o[l], xp.dtype)
        y = merge_branches(ret_b, diff_b, gr, gd, w_out[l])
        xp = xp + gt1 * rmsnorm(y, g_mix_post[l])
        h = modulate(xp, g_mlp_pre[l], sh2, sc2)
        xp = xp + gt2 * rmsnorm(channel_mlp(h, w_mlp_up[l], w_mlp_down[l]), g_mlp_post[l])
        new_k.append(dk)
        new_v.append(dv)
        new_sf.append(s_f.astype(xp.dtype))
        new_sb.append(s_b.astype(xp.dtype))
    y_prompt = xp
    new_cache_k = jnp.stack(new_k, axis=1)
    new_cache_v = jnp.stack(new_v, axis=1)
    new_state_fwd = jnp.stack(new_sf, axis=1)
    new_state_bwd = jnp.stack(new_sb, axis=1)

    xs = x_sample
    T = xs.shape[1]
    ang_r, ang_c = axial_rope_angles(T)
    for l in range(DEPTH):
        lam_init = 0.8 - 0.6 * math.exp(-0.3 * l)
        lam = diff_lambda(lambda_q1[l], lambda_k1[l], lambda_q2[l], lambda_k2[l], lam_init)
        lg_f = jax.nn.log_sigmoid(ret_gamma_logit_fwd[l].astype(F32))
        lg_b = jax.nn.log_sigmoid(ret_gamma_logit_bwd[l].astype(F32))
        sh1, sc1, gt1, sh2, sc2, gt2 = [m[:, None, :] for m in adaln(c, w_ada[l], b_ada[l])]
        h = modulate(xs, g_mix_pre[l], sh1, sc1)
        rq, rk, rv, rg, dq, dk, dv, gr, gd = project_in(h, w_in[l])
        ro, _, _ = retention_bidir(rq, rk, rv, lg_f, lg_b, state_ret_fwd[:, l], state_ret_bwd[:, l])
        ret_b = retention_out(ro, rg, w_ret_o[l])
        dq = apply_axial_rope(dq, ang_r, ang_c)
        dk = apply_axial_rope(dk, ang_r, ang_c)
        k_all = jnp.concatenate([dk, cache_k[:, l].astype(dk.dtype)], axis=1)
        v_all = jnp.concatenate([dv, cache_v[:, l].astype(dv.dtype)], axis=1)
        do = diff_attend_blocked(dq, k_all, v_all, lam)
        diff_b = diff_out(do, g_diff_subln[l], lam_init, w_diff_o[l], xs.dtype)
        y = merge_branches(ret_b, diff_b, gr, gd, w_out[l])
        xs = xs + gt1 * rmsnorm(y, g_mix_post[l])
        h = modulate(xs, g_mlp_pre[l], sh2, sc2)
        xs = xs + gt2 * rmsnorm(channel_mlp(h, w_mlp_up[l], w_mlp_down[l]), g_mlp_post[l])
    y_sample = xs
    return (y_prompt, y_sample, new_cache_k, new_cache_v, new_state_fwd, new_state_bwd)
```

```python
import functools
import math

import jax
import jax.numpy as jnp
from jax import lax
from jax.experimental import pallas as pl
from jax.experimental.pallas import tpu as pltpu

F32 = jnp.float32
BF16 = jnp.bfloat16

N_RET_HEADS = 8
N_DIFF_HEADS = 8
N_MOD = 6
GRID_W = 64
ROPE_BASE = 10000.0
EPS = 1e-6
SUBLN_EPS = 1e-5

V7X_LANES = 128
V7X_VMEM_BYTES = 64 * 1024 * 1024
VMEM_RESERVE_BYTES = 8 * 1024 * 1024


def _vmem_limit(block_bytes, scratch_bytes=0, temp_bytes=0):
    want = 2 * block_bytes + scratch_bytes + temp_bytes + VMEM_RESERVE_BYTES
    return int(min(want, V7X_VMEM_BYTES - VMEM_RESERVE_BYTES // 2))


def _nbytes(shape, dtype):
    return math.prod(shape) * jnp.dtype(dtype).itemsize


def _adaln_kernel(c_ref, w_ref, b_ref, o_ref):
    c = c_ref[...]
    s = (c * jax.nn.sigmoid(c)).astype(BF16)
    o_ref[...] = jnp.dot(s, w_ref[...].astype(BF16), preferred_element_type=F32) + b_ref[...]


def _adaln(cond, w_ada, b_ada, tn=512):
    R, D = cond.shape
    N = w_ada.shape[1]
    blocks = _nbytes((R, D), F32) + _nbytes((D, tn), F32) + _nbytes((1, tn), F32) + _nbytes((R, tn), F32)
    return pl.pallas_call(
        _adaln_kernel,
        out_shape=jax.ShapeDtypeStruct((R, N), F32),
        grid=(N // tn,),
        in_specs=[pl.BlockSpec((R, D), lambda j: (0, 0)),
                  pl.BlockSpec((D, tn), lambda j: (0, j)),
                  pl.BlockSpec((1, tn), lambda j: (0, j))],
        out_specs=pl.BlockSpec((R, tn), lambda j: (0, j)),
        compiler_params=pltpu.CompilerParams(
            dimension_semantics=("arbitrary",),
            vmem_limit_bytes=_vmem_limit(blocks, temp_bytes=_nbytes((D, tn), BF16))),
        name="adaln",
    )(cond, w_ada, b_ada)


ROW_CHUNK = 64


def _norm_modulate(x_ref, gain, shift, scale, h_ref):
    a = gain * (1.0 + scale)
    n = x_ref.shape[0] // ROW_CHUNK

    def body(r, carry):
        sl = pl.ds(pl.multiple_of(r * ROW_CHUNK, ROW_CHUNK), ROW_CHUNK)
        x = x_ref[sl, :]
        ms = jnp.mean(x * x, axis=-1, keepdims=True)
        h_ref[sl, :] = (x * lax.rsqrt(ms + EPS) * a + shift).astype(h_ref.dtype)
        return carry

    lax.fori_loop(0, n, body, 0)


def _inproj_kernel(x_ref, mod_ref, g_ref, w_ref, *rest, rope_blocks):
    if rope_blocks is None:
        o_ref, h_ref = rest
    else:
        cos_ref, sa_ref, sb_ref, o_ref, h_ref = rest
    j = pl.program_id(1)

    @pl.when(j == 0)
    def _():
        _norm_modulate(x_ref, g_ref[...], mod_ref[0:1, :], mod_ref[1:2, :], h_ref)

    acc = jnp.dot(h_ref[...], w_ref[...], preferred_element_type=F32)

    if rope_blocks is None:
        o_ref[...] = acc.astype(o_ref.dtype)
    else:
        j0, j1 = rope_blocks
        is_rope = jnp.logical_and(j >= j0, j < j1)

        @pl.when(is_rope)
        def _():
            cos, sa, sb = cos_ref[...], sa_ref[...], sb_ref[...]
            for c in range(acc.shape[1] // V7X_LANES):
                cs = slice(c * V7X_LANES, (c + 1) * V7X_LANES)
                xc = acc[:, cs]
                y = xc * cos + pltpu.roll(xc, 96, 1) * sa + pltpu.roll(xc, 32, 1) * sb
                o_ref[:, cs] = y.astype(o_ref.dtype)

        @pl.when(jnp.logical_not(is_rope))
        def _():
            o_ref[...] = acc.astype(o_ref.dtype)


def _inproj(x2, mod, gain, w, *, rows_per_group, out_dtype, rope=None, tm=1024, tn=1024, name="inproj"):
    M, D = x2.shape
    N = w.shape[1]
    tm = min(tm, M)
    tn = min(tn, N)
    n_groups = mod.shape[0]
    if rows_per_group is None:
        mod_map = lambda i, j: (n_groups - 1, 0, 0)
    else:
        mod_map = lambda i, j: ((i * tm) // rows_per_group, 0, 0)
    in_specs = [pl.BlockSpec((tm, D), lambda i, j: (i, 0)),
                pl.BlockSpec((None, N_MOD, D), mod_map),
                pl.BlockSpec((1, D), lambda i, j: (0, 0)),
                pl.BlockSpec((D, tn), lambda i, j: (0, j))]
    args = [x2, mod, gain, w]
    blocks = (_nbytes((tm, D), F32) + _nbytes((N_MOD, D), F32) + _nbytes((1, D), F32)
              + _nbytes((D, tn), BF16) + _nbytes((tm, tn), out_dtype))
    rope_blocks = None
    if rope is not None:
        tables, col0, col1 = rope
        T = tables[0].shape[0]
        nt = T // tm
        for t in tables:
            in_specs.append(pl.BlockSpec((tm, V7X_LANES), lambda i, j: (i % nt, 0)))
            args.append(t)
        blocks += 3 * _nbytes((tm, V7X_LANES), F32)
        rope_blocks = (col0 // tn, col1 // tn)
    return pl.pallas_call(
        functools.partial(_inproj_kernel, rope_blocks=rope_blocks),
        out_shape=jax.ShapeDtypeStruct((M, N), out_dtype),
        grid=(M // tm, N // tn),
        in_specs=in_specs,
        out_specs=pl.BlockSpec((tm, tn), lambda i, j: (i, j)),
        scratch_shapes=[pltpu.VMEM((tm, D), BF16)],
        compiler_params=pltpu.CompilerParams(
            dimension_semantics=("parallel", "arbitrary"),
            vmem_limit_bytes=_vmem_limit(blocks, _nbytes((tm, D), BF16), 2 * _nbytes((tm, tn), F32))),
        name=name,
    )(*args)


RET_CHUNK = 256


def _log_sigmoid(x):
    return jnp.minimum(x, 0.0) - jnp.log1p(jnp.exp(-jnp.abs(x)))


def _ret_kernel(lg_ref, q_ref, k_ref, v_ref, g_ref, *rest, has_state, emit_state, k_scale):
    rest = list(rest)
    s0f_ref = s0b_ref = sf_out = sb_out = None
    if has_state:
        s0f_ref, s0b_ref = rest[:2]
        rest = rest[2:]
    o_ref = rest[0]
    rest = rest[1:]
    if emit_state:
        sf_out, sb_out = rest[:2]
        rest = rest[2:]
    sf_scr, sb_scr, acc_scr = rest

    T = q_ref.shape[0]
    C = min(RET_CHUNK, T)
    n = T // C
    dv = v_ref.shape[1]

    lsig = _log_sigmoid(lg_ref[...])
    lgf = lsig[0:1, 0:1]
    lgb = lsig[1:2, 0:1]
    ri = lax.broadcasted_iota(jnp.int32, (C, C), 0)
    ci = lax.broadcasted_iota(jnp.int32, (C, C), 1)
    rel = (ri - ci).astype(F32)
    decay = jnp.where(rel > 0, jnp.exp(rel * lgf), jnp.where(rel < 0, jnp.exp(-rel * lgb), 2.0)) * k_scale
    li = lax.broadcasted_iota(jnp.int32, (C, 1), 0).astype(F32)
    xi_f = jnp.exp((li + 1.0) * lgf)
    xi_b = jnp.exp((C - li) * lgb)
    zeta_f = jnp.exp((C - 1.0 - li) * lgf) * k_scale
    zeta_b = jnp.exp(li * lgb) * k_scale
    gc_f = jnp.exp(C * lgf)
    gc_b = jnp.exp(C * lgb)

    if has_state:
        sf_scr[...] = s0f_ref[...]
        sb_scr[...] = s0b_ref[...]
    else:
        sf_scr[...] = jnp.zeros_like(sf_scr)
        sb_scr[...] = jnp.zeros_like(sb_scr)

    def rows(c):
        return pl.ds(pl.multiple_of(c * C, C), C)

    def kv_outer(kc, zeta, vc):
        ks = (kc.astype(F32) * zeta).astype(BF16)
        return lax.dot_general(ks, vc, (((0,), (0,)), ((), ())), preferred_element_type=F32)

    def bwd_body(t, carry):
        c = n - 1 - t
        sl = rows(c)
        qc, kc, vc = q_ref[sl, :], k_ref[sl, :], v_ref[sl, :]
        sb = sb_scr[...]
        acc_scr[sl, :] = xi_b * jnp.dot(qc, sb.astype(BF16), preferred_element_type=F32)
        sb_scr[...] = gc_b * sb + kv_outer(kc, zeta_b, vc)
        return carry

    lax.fori_loop(0, n, bwd_body, 0)

    def fwd_body(c, carry):
        sl = rows(c)
        qc, kc, vc = q_ref[sl, :], k_ref[sl, :], v_ref[sl, :]
        sf = sf_scr[...]
        s = lax.dot_general(qc, kc, (((1,), (1,)), ((), ())), preferred_element_type=F32) * decay
        o = (jnp.dot(s.astype(BF16), vc, preferred_element_type=F32)
             + xi_f * jnp.dot(qc, sf.astype(BF16), preferred_element_type=F32)
             + acc_scr[sl, :])
        sf_scr[...] = gc_f * sf + kv_outer(kc, zeta_f, vc)
        on = o * lax.rsqrt(jnp.mean(o * o, axis=-1, keepdims=True) + EPS)
        g = g_ref[sl, :].astype(F32)
        o_ref[sl, :] = (g * jax.nn.sigmoid(g) * on).astype(o_ref.dtype)
        return carry

    lax.fori_loop(0, n, fwd_body, 0)

    if emit_state:
        sf_out[...] = sf_scr[...]
        sb_out[...] = sb_scr[...]


def _retention(p3, lg, states, *, emit_state, dk, dv, off_q, off_k, off_v, off_g):
    B, T, _ = p3.shape
    H = N_RET_HEADS
    has_state = states is not None
    in_specs = [pl.BlockSpec((None, 8, V7X_LANES), lambda b, h: (h, 0, 0)),
                pl.BlockSpec((None, T, dk), lambda b, h: (b, 0, off_q // dk + h)),
                pl.BlockSpec((None, T, dk), lambda b, h: (b, 0, off_k // dk + h)),
                pl.BlockSpec((None, T, dv), lambda b, h: (b, 0, off_v // dv + h)),
                pl.BlockSpec((None, T, dv), lambda b, h: (b, 0, off_g // dv + h))]
    args = [lg, p3, p3, p3, p3]
    blocks = 2 * _nbytes((T, dk), BF16) + 3 * _nbytes((T, dv), BF16) + _nbytes((8, V7X_LANES), F32)
    state_spec = pl.BlockSpec((None, None, None, dk, dv), lambda b, h: (b, 0, h, 0, 0))
    if has_state:
        in_specs += [state_spec, state_spec]
        args += list(states)
        blocks += 2 * _nbytes((dk, dv), F32)
    out_shape = [jax.ShapeDtypeStruct((B, T, H * dv), BF16)]
    out_specs = [pl.BlockSpec((None, T, dv), lambda b, h: (b, 0, h))]
    if emit_state:
        out_shape += [jax.ShapeDtypeStruct((B, 1, H, dk, dv), F32)] * 2
        out_specs += [state_spec, state_spec]
        blocks += 2 * _nbytes((dk, dv), F32)
    scratch = [pltpu.VMEM((dk, dv), F32), pltpu.VMEM((dk, dv), F32), pltpu.VMEM((T, dv), F32)]
    scratch_bytes = 2 * _nbytes((dk, dv), F32) + _nbytes((T, dv), F32)
    C = min(RET_CHUNK, T)
    temps = 4 * _nbytes((C, dv), F32) + 4 * _nbytes((dk, dv), F32) + 4 * _nbytes((C, C), F32)
    return pl.pallas_call(
        functools.partial(_ret_kernel, has_state=has_state, emit_state=emit_state, k_scale=float(dk) ** -0.5),
        out_shape=out_shape,
        grid=(B, H),
        in_specs=in_specs,
        out_specs=out_specs,
        scratch_shapes=scratch,
        compiler_params=pltpu.CompilerParams(
            dimension_semantics=("parallel", "parallel"),
            vmem_limit_bytes=_vmem_limit(blocks, scratch_bytes, temps)),
        name="retention",
    )(*args)


def _diff_kernel(lam_ref, gs_ref, q_ref, k_ref, v_ref, *rest, has_cache, lam_init, dh):
    if has_cache:
        ck_ref, cv_ref, o_ref, kall, vall = rest
    else:
        o_ref, kall, vall = rest
    T = k_ref.shape[0]

    @pl.when(pl.program_id(2) == 0)
    def _():
        kall[0:T, :] = k_ref[...]
        vall[0:T, :] = v_ref[...]
        if has_cache:
            kall[T:, :] = ck_ref[...].astype(kall.dtype)
            vall[T:, :] = cv_ref[...].astype(vall.dtype)

    lp = lam_ref[...]
    lam = (jnp.exp(jnp.sum(lp[0:1, :] * lp[1:2, :], axis=-1, keepdims=True))
           - jnp.exp(jnp.sum(lp[2:3, :] * lp[3:4, :], axis=-1, keepdims=True)) + lam_init)

    scale = float(dh) ** -0.5
    q = q_ref[...]
    probs = []
    for m in range(2):
        cs = slice(m * dh, (m + 1) * dh)
        s = lax.dot_general(q[:, cs], kall[:, cs], (((1,), (1,)), ((), ())), preferred_element_type=F32)
        e = jnp.exp((s - jnp.max(s, axis=-1, keepdims=True)) * scale)
        probs.append((e, jnp.sum(e, axis=-1, keepdims=True)))
    (e1, l1), (e2, l2) = probs
    a = e1 * (1.0 / l1) - e2 * (lam / l2)
    o = jnp.dot(a.astype(BF16), vall[...], preferred_element_type=F32)
    on = o * lax.rsqrt(jnp.mean(o * o, axis=-1, keepdims=True) + SUBLN_EPS)
    o_ref[...] = (on * (gs_ref[...] * (1.0 - lam_init))).astype(o_ref.dtype)


def _diffattn(p3, lam_params, g_subln, caches, *, lam_init, dh, off_q, off_k, off_v, tq=256):
    B, T, _ = p3.shape
    H = N_DIFF_HEADS
    w = 2 * dh
    tq = min(tq, T)
    has_cache = caches is not None
    Tk = T + (caches[0].shape[1] if has_cache else 0)
    in_specs = [pl.BlockSpec((8, dh), lambda b, h, i: (0, 0)),
                pl.BlockSpec((1, w), lambda b, h, i: (0, 0)),
                pl.BlockSpec((None, tq, w), lambda b, h, i: (b, i, off_q // w + h)),
                pl.BlockSpec((None, T, w), lambda b, h, i: (b, 0, off_k // w + h)),
                pl.BlockSpec((None, T, w), lambda b, h, i: (b, 0, off_v // w + h))]
    args = [lam_params, g_subln, p3, p3, p3]
    blocks = _nbytes((tq, w), BF16) * 2 + 2 * _nbytes((T, w), BF16)
    if has_cache:
        P = caches[0].shape[1]
        cspec = pl.BlockSpec((None, P, w), lambda b, h, i: (b, 0, h))
        in_specs += [cspec, cspec]
        args += list(caches)
        blocks += 2 * _nbytes((P, w), F32)
    temps = 8 * _nbytes((tq, Tk), F32)
    return pl.pallas_call(
        functools.partial(_diff_kernel, has_cache=has_cache, lam_init=lam_init, dh=dh),
        out_shape=jax.ShapeDtypeStruct((B, T, H * w), BF16),
        grid=(B, H, T // tq),
        in_specs=in_specs,
        out_specs=pl.BlockSpec((None, tq, w), lambda b, h, i: (b, i, h)),
        scratch_shapes=[pltpu.VMEM((Tk, w), BF16), pltpu.VMEM((Tk, w), BF16)],
        compiler_params=pltpu.CompilerParams(
            dimension_semantics=("parallel", "parallel", "arbitrary"),
            vmem_limit_bytes=_vmem_limit(blocks, 2 * _nbytes((Tk, w), BF16), temps)),
        name="diffattn",
    )(*args)


def _merge_a_kernel(r_ref, d_ref, wr_ref, wd_ref, gr_ref, gd_ref, o_ref):
    rb = jnp.dot(r_ref[...], wr_ref[...], preferred_element_type=F32)
    db = jnp.dot(d_ref[...], wd_ref[...], preferred_element_type=F32)
    gr = jax.nn.sigmoid(gr_ref[...].astype(F32))
    gd = jax.nn.sigmoid(gd_ref[...].astype(F32))
    o_ref[...] = (gr * rb + gd * db).astype(o_ref.dtype)


def _merge_a(ret_g, diff_n, w_ret_o, w_diff_o, p2, *, off_gr, off_gd, tm=1024, tn=512):
    M, Kr = ret_g.shape
    Kd = diff_n.shape[1]
    D = w_ret_o.shape[1]
    tm = min(tm, M)
    blocks = (_nbytes((tm, Kr), BF16) + _nbytes((tm, Kd), BF16) + _nbytes((Kr, tn), BF16)
              + _nbytes((Kd, tn), BF16) + 3 * _nbytes((tm, tn), BF16))
    return pl.pallas_call(
        _merge_a_kernel,
        out_shape=jax.ShapeDtypeStruct((M, D), BF16),
        grid=(M // tm, D // tn),
        in_specs=[pl.BlockSpec((tm, Kr), lambda i, j: (i, 0)),
                  pl.BlockSpec((tm, Kd), lambda i, j: (i, 0)),
                  pl.BlockSpec((Kr, tn), lambda i, j: (0, j)),
                  pl.BlockSpec((Kd, tn), lambda i, j: (0, j)),
                  pl.BlockSpec((tm, tn), lambda i, j: (i, off_gr // tn + j)),
                  pl.BlockSpec((tm, tn), lambda i, j: (i, off_gd // tn + j))],
        out_specs=pl.BlockSpec((tm, tn), lambda i, j: (i, j)),
        compiler_params=pltpu.CompilerParams(
            dimension_semantics=("parallel", "arbitrary"),
            vmem_limit_bytes=_vmem_limit(blocks, temp_bytes=4 * _nbytes((tm, tn), F32))),
        name="merge_a",
    )(ret_g, diff_n, w_ret_o, w_diff_o, p2, p2)


def _merge_b_kernel(m_ref, w_ref, x_ref, mod_ref, gpost_ref, gpre_ref, x1_ref, h2_ref, y_scr):
    y_scr[...] = jnp.dot(m_ref[...], w_ref[...], preferred_element_type=F32)
    gate = mod_ref[2:3, :]
    a2 = gpre_ref[...] * (1.0 + mod_ref[4:5, :])
    shift2 = mod_ref[3:4, :]
    gpost = gpost_ref[...]
    n = x_ref.shape[0] // ROW_CHUNK

    def body(r, carry):
        sl = pl.ds(pl.multiple_of(r * ROW_CHUNK, ROW_CHUNK), ROW_CHUNK)
        y = y_scr[sl, :]
        yn = y * lax.rsqrt(jnp.mean(y * y, axis=-1, keepdims=True) + EPS) * gpost
        x1 = x_ref[sl, :] + gate * yn
        x1_ref[sl, :] = x1
        h2_ref[sl, :] = (x1 * lax.rsqrt(jnp.mean(x1 * x1, axis=-1, keepdims=True) + EPS) * a2
                         + shift2).astype(h2_ref.dtype)
        return carry

    lax.fori_loop(0, n, body, 0)


def _merge_b(merged, w_out, x2, mod, g_post, g_pre, *, rows_per_group, tm=512):
    M, D = x2.shape
    tm = min(tm, M)
    n_groups = mod.shape[0]
    if rows_per_group is None:
        mod_map = lambda i: (n_groups - 1, 0, 0)
    else:
        mod_map = lambda i: ((i * tm) // rows_per_group, 0, 0)
    blocks = (_nbytes((tm, D), BF16) * 2 + _nbytes((D, D), BF16) + 2 * _nbytes((tm, D), F32)
              + _nbytes((N_MOD + 2, D), F32))
    return pl.pallas_call(
        _merge_b_kernel,
        out_shape=[jax.ShapeDtypeStruct((M, D), F32), jax.ShapeDtypeStruct((M, D), BF16)],
        grid=(M // tm,),
        in_specs=[pl.BlockSpec((tm, D), lambda i: (i, 0)),
                  pl.BlockSpec((D, D), lambda i: (0, 0)),
                  pl.BlockSpec((tm, D), lambda i: (i, 0)),
                  pl.BlockSpec((None, N_MOD, D), mod_map),
                  pl.BlockSpec((1, D), lambda i: (0, 0)),
                  pl.BlockSpec((1, D), lambda i: (0, 0))],
        out_specs=[pl.BlockSpec((tm, D), lambda i: (i, 0)),
                   pl.BlockSpec((tm, D), lambda i: (i, 0))],
        scratch_shapes=[pltpu.VMEM((tm, D), F32)],
        compiler_params=pltpu.CompilerParams(
            dimension_semantics=("parallel",),
            vmem_limit_bytes=_vmem_limit(blocks, _nbytes((tm, D), F32), _nbytes((tm, D), F32))),
        name="merge_b",
    )(merged, w_out, x2, mod, g_post, g_pre)


def _mlp_kernel(h_ref, wu_ref, wd_ref, x_ref, mod_ref, g_ref, o_ref, acc_ref):
    j = pl.program_id(1)
    u = jnp.maximum(jnp.dot(h_ref[...], wu_ref[...], preferred_element_type=F32), 0.0)
    part = jnp.dot((u * u).astype(BF16), wd_ref[...], preferred_element_type=F32)

    @pl.when(j == 0)
    def _():
        acc_ref[...] = part

    @pl.when(j > 0)
    def _():
        acc_ref[...] += part

    @pl.when(j == pl.num_programs(1) - 1)
    def _():
        gate = mod_ref[5:6, :]
        gpost = g_ref[...]
        n = x_ref.shape[0] // ROW_CHUNK

        def body(r, carry):
            sl = pl.ds(pl.multiple_of(r * ROW_CHUNK, ROW_CHUNK), ROW_CHUNK)
            y = acc_ref[sl, :]
            yn = y * lax.rsqrt(jnp.mean(y * y, axis=-1, keepdims=True) + EPS) * gpost
            o_ref[sl, :] = x_ref[sl, :] + gate * yn
            return carry

        lax.fori_loop(0, n, body, 0)


def _mlp(h2, w_up, w_down, x1, mod, g_post, *, rows_per_group, tm=512, tf=1024):
    M, D = x1.shape
    Fh = w_up.shape[1]
    tm = min(tm, M)
    n_groups = mod.shape[0]
    if rows_per_group is None:
        mod_map = lambda i, j: (n_groups - 1, 0, 0)
    else:
        mod_map = lambda i, j: ((i * tm) // rows_per_group, 0, 0)
    blocks = (_nbytes((tm, D), BF16) + 2 * _nbytes((D, tf), BF16) + 2 * _nbytes((tm, D), F32)
              + _nbytes((N_MOD + 1, D), F32))
    return pl.pallas_call(
        _mlp_kernel,
        out_shape=jax.ShapeDtypeStruct((M, D), F32),
        grid=(M // tm, Fh // tf),
        in_specs=[pl.BlockSpec((tm, D), lambda i, j: (i, 0)),
                  pl.BlockSpec((D, tf), lambda i, j: (0, j)),
                  pl.BlockSpec((tf, D), lambda i, j: (j, 0)),
                  pl.BlockSpec((tm, D), lambda i, j: (i, 0)),
                  pl.BlockSpec((None, N_MOD, D), mod_map),
                  pl.BlockSpec((1, D), lambda i, j: (0, 0))],
        out_specs=pl.BlockSpec((tm, D), lambda i, j: (i, 0)),
        scratch_shapes=[pltpu.VMEM((tm, D), F32)],
        compiler_params=pltpu.CompilerParams(
            dimension_semantics=("parallel", "arbitrary"),
            vmem_limit_bytes=_vmem_limit(blocks, _nbytes((tm, D), F32),
                                         2 * _nbytes((tm, tf), F32) + _nbytes((tm, D), F32))),
        name="mlp",
    )(h2, w_up, w_down, x1, mod, g_post)


def _rope_tables(T, rope_half):
    rows = T // GRID_W
    row = jnp.repeat(jnp.arange(rows, dtype=F32), GRID_W)
    col = jnp.tile(jnp.arange(GRID_W, dtype=F32), rows)
    inv = ROPE_BASE ** (-jnp.arange(0, rope_half, 2, dtype=F32) / rope_half)
    ar, ac = row[:, None] * inv, col[:, None] * inv
    z = jnp.zeros_like(ar)
    cos = jnp.concatenate([jnp.cos(ar), jnp.cos(ar), jnp.cos(ac), jnp.cos(ac)], axis=-1)
    sin_next = jnp.concatenate([-jnp.sin(ar), z, -jnp.sin(ac), z], axis=-1)
    sin_prev = jnp.concatenate([z, jnp.sin(ar), z, jnp.sin(ac)], axis=-1)
    return cos, sin_next, sin_prev


def kernel(x_prompt, x_sample, cache_k, cache_v, state_ret_fwd, state_ret_bwd, c, c_ctx, w_ada, b_ada, g_mix_pre, g_mix_post, g_mlp_pre, g_mlp_post, w_in, ret_gamma_logit_fwd, ret_gamma_logit_bwd, w_ret_o, lambda_q1, lambda_k1, lambda_q2, lambda_k2, g_diff_subln, w_diff_o, w_out, w_mlp_up, w_mlp_down):
    Bp, Tp, D = x_prompt.shape
    Bs, Ts, _ = x_sample.shape
    depth = w_in.shape[0]
    assert depth == 1
    l = 0
    H = N_RET_HEADS
    dk = D // H
    dv = 2 * dk
    dh = D // N_DIFF_HEADS // 2
    ret_qk_w, ret_v_w, diff_w, gate_w = H * dk, H * dv, N_DIFF_HEADS * 2 * dh, 2 * D
    splits = (ret_qk_w, ret_qk_w, ret_v_w, ret_v_w, diff_w, diff_w, diff_w, gate_w)
    offs = [0]
    for s in splits:
        offs.append(offs[-1] + s)
    o_rq, o_rk, o_rv, o_rg, o_dq, o_dk, o_dv, o_gate = offs[:8]
    in_w = offs[8]
    assert w_in.shape[2] == in_w
    lam_init = 0.8 - 0.6 * math.exp(-0.3 * l)

    w_in_b = w_in[l].astype(BF16)
    w_ret_o_b = w_ret_o[l].astype(BF16)
    w_diff_o_b = w_diff_o[l].astype(BF16)
    w_out_b = w_out[l].astype(BF16)
    w_up_b = w_mlp_up[l].astype(BF16)
    w_down_b = w_mlp_down[l].astype(BF16)
    g_mix_pre_l = g_mix_pre[l].reshape(1, D)
    g_mix_post_l = g_mix_post[l].reshape(1, D)
    g_mlp_pre_l = g_mlp_pre[l].reshape(1, D)
    g_mlp_post_l = g_mlp_post[l].reshape(1, D)
    g_subln_l = g_diff_subln[l].reshape(1, 2 * dh)
    lam_params = jnp.concatenate(
        [jnp.stack([lambda_q1[l], lambda_k1[l], lambda_q2[l], lambda_k2[l]]), jnp.zeros((4, dh), F32)], axis=0)
    lg = jnp.stack([ret_gamma_logit_fwd[l], ret_gamma_logit_bwd[l]], axis=1)
    lg = jnp.concatenate([lg, jnp.zeros((H, 6), F32)], axis=1)
    lg = jnp.broadcast_to(lg[:, :, None], (H, 8, V7X_LANES))

    n_cond = 16
    cond = jnp.concatenate([c, jnp.broadcast_to(c_ctx[None, :], (n_cond - Bs, D))], axis=0)
    mod = _adaln(cond, w_ada[l], b_ada[l].reshape(1, N_MOD * D)).reshape(n_cond, N_MOD, D)

    def trunk(x, rows_per_group, rope, states, caches, emit):
        B, T, _ = x.shape
        x2 = x.reshape(B * T, D)
        p2 = _inproj(x2, mod, g_mix_pre_l, w_in_b, rows_per_group=rows_per_group, out_dtype=BF16,
                     rope=rope, name="inproj")
        p3 = p2.reshape(B, T, in_w)
        ret = _retention(p3, lg, states, emit_state=emit, dk=dk, dv=dv,
                         off_q=o_rq, off_k=o_rk, off_v=o_rv, off_g=o_rg)
        ret_g = ret[0] if emit else ret[0]
        diff_n = _diffattn(p3, lam_params, g_subln_l, caches, lam_init=lam_init, dh=dh,
                           off_q=o_dq, off_k=o_dk, off_v=o_dv)
        merged = _merge_a(ret_g.reshape(B * T, ret_v_w), diff_n.reshape(B * T, diff_w), w_ret_o_b, w_diff_o_b,
                          p2, off_gr=o_gate, off_gd=o_gate + D)
        x1, h2 = _merge_b(merged, w_out_b, x2, mod, g_mix_post_l, g_mlp_pre_l, rows_per_group=rows_per_group)
        y = _mlp(h2, w_up_b, w_down_b, x1, mod, g_mlp_post_l, rows_per_group=rows_per_group)
        return y.reshape(B, T, D), ret, x2

    y_prompt, ret_p, xp2 = trunk(x_prompt, None, None, None, None, True)
    new_state_fwd, new_state_bwd = ret_p[1], ret_p[2]
    kv = _inproj(xp2, mod, g_mix_pre_l, w_in_b[:, o_dk:o_dv + diff_w], rows_per_group=None, out_dtype=F32,
                 name="inproj_kv")
    new_cache_k = kv[:, :diff_w].reshape(Bp, 1, Tp, N_DIFF_HEADS, 2, dh)
    new_cache_v = kv[:, diff_w:].reshape(Bp, 1, Tp, N_DIFF_HEADS, 2 * dh)

    tables = _rope_tables(Ts, dh // 2)
    caches = (cache_k[:, l].reshape(Bs, -1, diff_w), cache_v[:, l].reshape(Bs, -1, diff_w))
    y_sample, _, _ = trunk(x_sample, Ts, (tables, o_dq, o_dv), (state_ret_fwd, state_ret_bwd), caches, False)

    return (y_prompt, y_sample, new_cache_k, new_cache_v, new_state_fwd, new_state_bwd)
```

```python
import functools
import math

import jax
import jax.numpy as jnp
from jax import lax
from jax.experimental import pallas as pl
from jax.experimental.pallas import tpu as pltpu

F32 = jnp.float32
BF16 = jnp.bfloat16

N_RET_HEADS = 8
N_DIFF_HEADS = 8
N_MOD = 6
GRID_W = 64
ROPE_BASE = 10000.0
EPS = 1e-6
SUBLN_EPS = 1e-5

V7X_LANES = 128
V7X_VMEM_BYTES = 64 * 1024 * 1024
VMEM_RESERVE_BYTES = 8 * 1024 * 1024


def _vmem_limit(block_bytes, scratch_bytes=0, temp_bytes=0):
    want = 2 * block_bytes + scratch_bytes + temp_bytes + VMEM_RESERVE_BYTES
    return int(min(want, V7X_VMEM_BYTES - VMEM_RESERVE_BYTES // 2))


def _nbytes(shape, dtype):
    return math.prod(shape) * jnp.dtype(dtype).itemsize


def _mod_map(n_groups, rows_per_group, tm):
    if rows_per_group is None:
        return lambda i, *_: (n_groups - 1, 0, 0)
    return lambda i, *_: ((i * tm) // rows_per_group, 0, 0)


def _adaln_kernel(c_ref, w_ref, b_ref, o_ref):
    c = c_ref[...]
    s = (c * jax.nn.sigmoid(c)).astype(BF16)
    o_ref[...] = jnp.dot(s, w_ref[...].astype(BF16), preferred_element_type=F32) + b_ref[...]


def _adaln(cond, w_ada, b_ada, tn=512):
    R, D = cond.shape
    N = w_ada.shape[1]
    blocks = _nbytes((R, D), F32) + _nbytes((D, tn), F32) + _nbytes((1, tn), F32) + _nbytes((R, tn), F32)
    return pl.pallas_call(
        _adaln_kernel,
        out_shape=jax.ShapeDtypeStruct((R, N), F32),
        grid=(N // tn,),
        in_specs=[pl.BlockSpec((R, D), lambda j: (0, 0)),
                  pl.BlockSpec((D, tn), lambda j: (0, j)),
                  pl.BlockSpec((1, tn), lambda j: (0, j))],
        out_specs=pl.BlockSpec((R, tn), lambda j: (0, j)),
        compiler_params=pltpu.CompilerParams(
            dimension_semantics=("arbitrary",),
            vmem_limit_bytes=_vmem_limit(blocks, temp_bytes=_nbytes((D, tn), BF16))),
        name="adaln",
    )(cond, w_ada, b_ada)


ROW_CHUNK = 64


def _norm_modulate(x_ref, gain, shift, scale, h_ref):
    a = gain * (1.0 + scale)
    n = x_ref.shape[0] // ROW_CHUNK

    def body(r, carry):
        sl = pl.ds(pl.multiple_of(r * ROW_CHUNK, ROW_CHUNK), ROW_CHUNK)
        x = x_ref[sl, :]
        ms = jnp.mean(x * x, axis=-1, keepdims=True)
        h_ref[sl, :] = (x * lax.rsqrt(ms + EPS) * a + shift).astype(h_ref.dtype)
        return carry

    lax.fori_loop(0, n, body, 0)


def _prenorm_kernel(x_ref, mod_ref, g_ref, h_ref):
    _norm_modulate(x_ref, g_ref[...], mod_ref[0:1, :], mod_ref[1:2, :], h_ref)


def _prenorm(x2, mod, gain, *, rows_per_group, tm=512):
    M, D = x2.shape
    blocks = _nbytes((tm, D), F32) + _nbytes((tm, D), BF16) + _nbytes((N_MOD + 1, D), F32)
    return pl.pallas_call(
        _prenorm_kernel,
        out_shape=jax.ShapeDtypeStruct((M, D), BF16),
        grid=(M // tm,),
        in_specs=[pl.BlockSpec((tm, D), lambda i: (i, 0)),
                  pl.BlockSpec((None, N_MOD, D), _mod_map(mod.shape[0], rows_per_group, tm)),
                  pl.BlockSpec((1, D), lambda i: (0, 0))],
        out_specs=pl.BlockSpec((tm, D), lambda i: (i, 0)),
        compiler_params=pltpu.CompilerParams(
            dimension_semantics=("parallel",),
            vmem_limit_bytes=_vmem_limit(blocks, temp_bytes=4 * _nbytes((ROW_CHUNK, D), F32))),
        name="prenorm",
    )(x2, mod, gain)


def _proj_kernel(h_ref, w_ref, *rest, n_q_blocks, q_mult, rope):
    if rope:
        cos_ref, sa_ref, sb_ref, o_ref = rest
    else:
        (o_ref,) = rest
    acc = jnp.dot(h_ref[...], w_ref[...], preferred_element_type=F32)
    mult = None
    if n_q_blocks:
        mult = jnp.where(pl.program_id(1) < n_q_blocks, q_mult, 1.0).astype(F32)
    if rope:
        cos, sa, sb = cos_ref[...] * mult, sa_ref[...] * mult, sb_ref[...] * mult
        for c in range(acc.shape[1] // V7X_LANES):
            cs = slice(c * V7X_LANES, (c + 1) * V7X_LANES)
            xc = acc[:, cs]
            y = xc * cos + pltpu.roll(xc, 96, 1) * sa + pltpu.roll(xc, 32, 1) * sb
            o_ref[:, cs] = y.astype(o_ref.dtype)
    elif mult is not None:
        o_ref[...] = (acc * mult).astype(o_ref.dtype)
    else:
        o_ref[...] = acc.astype(o_ref.dtype)


def _proj(h, w, *, out_dtype, n_q_cols=0, q_mult=1.0, rope_tables=None, tm=1024, tn=1024, name="proj"):
    M, D = h.shape
    N = w.shape[1]
    tm = min(tm, M)
    tn = min(tn, N)
    in_specs = [pl.BlockSpec((tm, D), lambda i, j: (i, 0)),
                pl.BlockSpec((D, tn), lambda i, j: (0, j))]
    args = [h, w]
    blocks = _nbytes((tm, D), BF16) + _nbytes((D, tn), BF16) + _nbytes((tm, tn), out_dtype)
    if rope_tables is not None:
        nt = rope_tables[0].shape[0] // tm
        for t in rope_tables:
            in_specs.append(pl.BlockSpec((tm, V7X_LANES), lambda i, j: (i % nt, 0)))
            args.append(t)
        blocks += 3 * _nbytes((tm, V7X_LANES), F32)
    return pl.pallas_call(
        functools.partial(_proj_kernel, n_q_blocks=n_q_cols // tn, q_mult=q_mult, rope=rope_tables is not None),
        out_shape=jax.ShapeDtypeStruct((M, N), out_dtype),
        grid=(M // tm, N // tn),
        in_specs=in_specs,
        out_specs=pl.BlockSpec((tm, tn), lambda i, j: (i, j)),
        compiler_params=pltpu.CompilerParams(
            dimension_semantics=("parallel", "arbitrary"),
            vmem_limit_bytes=_vmem_limit(blocks, temp_bytes=2 * _nbytes((tm, tn), F32))),
        name=name,
    )(*args)


RET_CHUNK = 256


def _log_sigmoid(x):
    return jnp.minimum(x, 0.0) - jnp.log1p(jnp.exp(-jnp.abs(x)))


def _ret_kernel(lg_ref, q_ref, k_ref, v_ref, g_ref, *rest, has_state, emit_state, k_scale):
    rest = list(rest)
    s0f_ref = s0b_ref = sf_out = sb_out = None
    if has_state:
        s0f_ref, s0b_ref = rest[:2]
        rest = rest[2:]
    o_ref = rest[0]
    rest = rest[1:]
    if emit_state:
        sf_out, sb_out = rest[:2]
        rest = rest[2:]
    sf_scr, sb_scr, st_scr = rest

    T, dk = q_ref.shape
    C = min(RET_CHUNK, T)
    n = T // C

    lsig = _log_sigmoid(lg_ref[...])
    lgf = lsig[0:1, 0:1]
    lgb = lsig[1:2, 0:1]
    ri = lax.broadcasted_iota(jnp.int32, (C, C), 0)
    ci = lax.broadcasted_iota(jnp.int32, (C, C), 1)
    rel = (ri - ci).astype(F32)
    decay = jnp.where(rel > 0, jnp.exp(rel * lgf), jnp.where(rel < 0, jnp.exp(-rel * lgb), 2.0)) * k_scale
    li = lax.broadcasted_iota(jnp.int32, (C, 1), 0).astype(F32)
    xi_f = jnp.exp((li + 1.0) * lgf)
    xi_b = jnp.exp((C - li) * lgb)
    zeta_f = jnp.exp((C - 1.0 - li) * lgf) * k_scale
    zeta_b = jnp.exp(li * lgb) * k_scale
    gc_f = jnp.exp(C * lgf)
    gc_b = jnp.exp(C * lgb)

    if has_state:
        sf_scr[...] = s0f_ref[...]
        sb_scr[...] = s0b_ref[...]
    else:
        sf_scr[...] = jnp.zeros_like(sf_scr)
        sb_scr[...] = jnp.zeros_like(sb_scr)

    def rows(c):
        return slice(c * C, (c + 1) * C)

    def kv_outer(c, zeta):
        ks = (k_ref[rows(c), :].astype(F32) * zeta).astype(BF16)
        return lax.dot_general(ks, v_ref[rows(c), :], (((0,), (0,)), ((), ())), preferred_element_type=F32)

    n_upd = n if emit_state else n - 1
    kv_next = (kv_outer(0, zeta_f), kv_outer(n - 1, zeta_b)) if n_upd > 0 else None
    for t in range(n):
        cf, cb = t, n - 1 - t
        st_scr[cf, 0:dk, :] = sf_scr[...].astype(BF16)
        st_scr[cb, dk:2 * dk, :] = sb_scr[...].astype(BF16)
        if t < n_upd:
            kvf, kvb = kv_next
            if t + 1 < n_upd:
                kv_next = (kv_outer(cf + 1, zeta_f), kv_outer(cb - 1, zeta_b))
            sf_scr[...] = gc_f * sf_scr[...] + kvf
            sb_scr[...] = gc_b * sb_scr[...] + kvb

    if emit_state:
        sf_out[...] = sf_scr[...]
        sb_out[...] = sb_scr[...]

    def chunk_out(c):
        qc = q_ref[rows(c), :]
        s = lax.dot_general(qc, k_ref[rows(c), :], (((1,), (1,)), ((), ())), preferred_element_type=F32) * decay
        qf = qc.astype(F32)
        qx = jnp.concatenate([(qf * xi_f).astype(BF16), (qf * xi_b).astype(BF16)], axis=1)
        return (jnp.dot(s.astype(BF16), v_ref[rows(c), :], preferred_element_type=F32)
                + jnp.dot(qx, st_scr[c], preferred_element_type=F32))

    o_next = chunk_out(0)
    for c in range(n):
        o = o_next
        if c + 1 < n:
            o_next = chunk_out(c + 1)
        on = o * lax.rsqrt(jnp.mean(o * o, axis=-1, keepdims=True) + EPS)
        g = g_ref[rows(c), :].astype(F32)
        o_ref[rows(c), :] = (g * jax.nn.sigmoid(g) * on).astype(o_ref.dtype)


def _retention(pa3, lg, states, *, emit_state, dk, dv, off_q, off_k, off_v, off_g):
    B, T, _ = pa3.shape
    H = N_RET_HEADS
    has_state = states is not None
    C = min(RET_CHUNK, T)
    n = T // C
    in_specs = [pl.BlockSpec((None, 8, V7X_LANES), lambda b, h: (h, 0, 0)),
                pl.BlockSpec((None, T, dk), lambda b, h: (b, 0, off_q // dk + h)),
                pl.BlockSpec((None, T, dk), lambda b, h: (b, 0, off_k // dk + h)),
                pl.BlockSpec((None, T, dv), lambda b, h: (b, 0, off_v // dv + h)),
                pl.BlockSpec((None, T, dv), lambda b, h: (b, 0, off_g // dv + h))]
    args = [lg, pa3, pa3, pa3, pa3]
    blocks = 2 * _nbytes((T, dk), BF16) + 3 * _nbytes((T, dv), BF16) + _nbytes((8, V7X_LANES), F32)
    state_spec = pl.BlockSpec((None, None, None, dk, dv), lambda b, h: (b, 0, h, 0, 0))
    if has_state:
        in_specs += [state_spec, state_spec]
        args += list(states)
        blocks += 2 * _nbytes((dk, dv), F32)
    out_shape = [jax.ShapeDtypeStruct((B, T, H * dv), BF16)]
    out_specs = [pl.BlockSpec((None, T, dv), lambda b, h: (b, 0, h))]
    if emit_state:
        out_shape += [jax.ShapeDtypeStruct((B, 1, H, dk, dv), F32)] * 2
        out_specs += [state_spec, state_spec]
        blocks += 2 * _nbytes((dk, dv), F32)
    scratch = [pltpu.VMEM((dk, dv), F32), pltpu.VMEM((dk, dv), F32), pltpu.VMEM((n, 2 * dk, dv), BF16)]
    scratch_bytes = 2 * _nbytes((dk, dv), F32) + _nbytes((n, 2 * dk, dv), BF16)
    temps = 8 * _nbytes((C, dv), F32) + 6 * _nbytes((dk, dv), F32) + 4 * _nbytes((C, C), F32)
    return pl.pallas_call(
        functools.partial(_ret_kernel, has_state=has_state, emit_state=emit_state, k_scale=float(dk) ** -0.5),
        out_shape=out_shape,
        grid=(B, H),
        in_specs=in_specs,
        out_specs=out_specs,
        scratch_shapes=scratch,
        compiler_params=pltpu.CompilerParams(
            dimension_semantics=("parallel", "parallel"),
            vmem_limit_bytes=_vmem_limit(blocks, scratch_bytes, temps)),
        name="retention",
    )(*args)


DIFF_SUB_ROWS = 256


def _diff_kernel(lam_ref, gs_ref, q_ref, k_ref, v_ref, *rest, has_cache, lam_init, dh, sub):
    if has_cache:
        ck_ref, cv_ref, o_ref, kall, vall = rest
    else:
        o_ref, kall, vall = rest
    T = k_ref.shape[0]

    @pl.when(pl.program_id(2) == 0)
    def _():
        kall[0:T, :] = k_ref[...].astype(kall.dtype)
        vall[0:T, :] = v_ref[...].astype(vall.dtype)
        if has_cache:
            kall[T:, :] = ck_ref[...].astype(kall.dtype)
            vall[T:, :] = cv_ref[...].astype(vall.dtype)

    lp = lam_ref[...]
    lam = (jnp.exp(jnp.sum(lp[0:1, :] * lp[1:2, :], axis=-1, keepdims=True))
           - jnp.exp(jnp.sum(lp[2:3, :] * lp[3:4, :], axis=-1, keepdims=True)) + lam_init)
    gain = gs_ref[...] * (1.0 - lam_init)
    n_sub = q_ref.shape[0] // sub

    def scores(r):
        q = q_ref[r * sub:(r + 1) * sub, :]
        return [lax.dot_general(q[:, m * dh:(m + 1) * dh], kall[:, m * dh:(m + 1) * dh],
                                (((1,), (1,)), ((), ())), preferred_element_type=F32) for m in range(2)]

    s_next = scores(0)
    for r in range(n_sub):
        s_cur = s_next
        if r + 1 < n_sub:
            s_next = scores(r + 1)
        probs = []
        for s in s_cur:
            e = jnp.exp2(s - jnp.max(s, axis=-1, keepdims=True))
            probs.append((e, jnp.sum(e, axis=-1, keepdims=True)))
        (e1, l1), (e2, l2) = probs
        r1 = 1.0 / l1
        a = (e1 - e2 * (lam * l1 / l2)).astype(BF16)
        o = jnp.dot(a, vall[...], preferred_element_type=F32) * r1
        on = o * lax.rsqrt(jnp.mean(o * o, axis=-1, keepdims=True) + SUBLN_EPS)
        o_ref[r * sub:(r + 1) * sub, :] = (on * gain).astype(o_ref.dtype)


def _diffattn(q_src, k_src, v_src, lam_params, g_subln, caches, *, lam_init, dh, tq=1024):
    (qa, off_q), (ka, off_k), (va, off_v) = q_src, k_src, v_src
    B, T, _ = qa.shape
    H = N_DIFF_HEADS
    w = 2 * dh
    tq = min(tq, T)
    sub = min(DIFF_SUB_ROWS, tq)
    has_cache = caches is not None
    Tk = T + (caches[0].shape[1] if has_cache else 0)
    in_specs = [pl.BlockSpec((8, dh), lambda b, h, i: (0, 0)),
                pl.BlockSpec((1, w), lambda b, h, i: (0, 0)),
                pl.BlockSpec((None, tq, w), lambda b, h, i: (b, i, off_q // w + h)),
                pl.BlockSpec((None, T, w), lambda b, h, i: (b, 0, off_k // w + h)),
                pl.BlockSpec((None, T, w), lambda b, h, i: (b, 0, off_v // w + h))]
    args = [lam_params, g_subln, qa, ka, va]
    blocks = 2 * _nbytes((tq, w), BF16) + _nbytes((T, w), ka.dtype) + _nbytes((T, w), va.dtype)
    if has_cache:
        P = caches[0].shape[1]
        cspec = pl.BlockSpec((None, P, w), lambda b, h, i: (b, 0, h))
        in_specs += [cspec, cspec]
        args += list(caches)
        blocks += 2 * _nbytes((P, w), F32)
    temps = 10 * _nbytes((sub, Tk), F32)
    return pl.pallas_call(
        functools.partial(_diff_kernel, has_cache=has_cache, lam_init=lam_init, dh=dh, sub=sub),
        out_shape=jax.ShapeDtypeStruct((B, T, H * w), BF16),
        grid=(B, H, T // tq),
        in_specs=in_specs,
        out_specs=pl.BlockSpec((None, tq, w), lambda b, h, i: (b, i, h)),
        scratch_shapes=[pltpu.VMEM((Tk, w), BF16), pltpu.VMEM((Tk, w), BF16)],
        compiler_params=pltpu.CompilerParams(
            dimension_semantics=("parallel", "parallel", "arbitrary"),
            vmem_limit_bytes=_vmem_limit(blocks, 2 * _nbytes((Tk, w), BF16), temps)),
        name="diffattn",
    )(*args)


def _merge_a_kernel(r_ref, d_ref, wr_ref, wd_ref, gr_ref, gd_ref, o_ref):
    rb = jnp.dot(r_ref[...], wr_ref[...], preferred_element_type=F32)
    db = jnp.dot(d_ref[...], wd_ref[...], preferred_element_type=F32)
    gr = jax.nn.sigmoid(gr_ref[...].astype(F32))
    gd = jax.nn.sigmoid(gd_ref[...].astype(F32))
    o_ref[...] = (gr * rb + gd * db).astype(o_ref.dtype)


def _merge_a(ret_g, diff_n, w_ret_o, w_diff_o, gates, *, off_gr, off_gd, tm=1024, tn=512):
    M, Kr = ret_g.shape
    Kd = diff_n.shape[1]
    D = w_ret_o.shape[1]
    tm = min(tm, M)
    blocks = (_nbytes((tm, Kr), BF16) + _nbytes((tm, Kd), BF16) + _nbytes((Kr, tn), BF16)
              + _nbytes((Kd, tn), BF16) + 3 * _nbytes((tm, tn), BF16))
    return pl.pallas_call(
        _merge_a_kernel,
        out_shape=jax.ShapeDtypeStruct((M, D), BF16),
        grid=(M // tm, D // tn),
        in_specs=[pl.BlockSpec((tm, Kr), lambda i, j: (i, 0)),
                  pl.BlockSpec((tm, Kd), lambda i, j: (i, 0)),
                  pl.BlockSpec((Kr, tn), lambda i, j: (0, j)),
                  pl.BlockSpec((Kd, tn), lambda i, j: (0, j)),
                  pl.BlockSpec((tm, tn), lambda i, j: (i, off_gr // tn + j)),
                  pl.BlockSpec((tm, tn), lambda i, j: (i, off_gd // tn + j))],
        out_specs=pl.BlockSpec((tm, tn), lambda i, j: (i, j)),
        compiler_params=pltpu.CompilerParams(
            dimension_semantics=("parallel", "arbitrary"),
            vmem_limit_bytes=_vmem_limit(blocks, temp_bytes=4 * _nbytes((tm, tn), F32))),
        name="merge_a",
    )(ret_g, diff_n, w_ret_o, w_diff_o, gates, gates)


def _merge_b_kernel(m_ref, w_ref, x_ref, mod_ref, gpost_ref, gpre_ref, x1_ref, h2_ref, y_scr):
    y_scr[...] = jnp.dot(m_ref[...], w_ref[...], preferred_element_type=F32)
    gate = mod_ref[2:3, :]
    a2 = gpre_ref[...] * (1.0 + mod_ref[4:5, :])
    shift2 = mod_ref[3:4, :]
    gpost = gpost_ref[...]
    n = x_ref.shape[0] // ROW_CHUNK

    def body(r, carry):
        sl = pl.ds(pl.multiple_of(r * ROW_CHUNK, ROW_CHUNK), ROW_CHUNK)
        y = y_scr[sl, :]
        yn = y * lax.rsqrt(jnp.mean(y * y, axis=-1, keepdims=True) + EPS) * gpost
        x1 = x_ref[sl, :] + gate * yn
        x1_ref[sl, :] = x1
        h2_ref[sl, :] = (x1 * lax.rsqrt(jnp.mean(x1 * x1, axis=-1, keepdims=True) + EPS) * a2
                         + shift2).astype(h2_ref.dtype)
        return carry

    lax.fori_loop(0, n, body, 0)


def _merge_b(merged, w_out, x2, mod, g_post, g_pre, *, rows_per_group, tm=512):
    M, D = x2.shape
    tm = min(tm, M)
    blocks = (_nbytes((tm, D), BF16) * 2 + _nbytes((D, D), BF16) + 2 * _nbytes((tm, D), F32)
              + _nbytes((N_MOD + 2, D), F32))
    return pl.pallas_call(
        _merge_b_kernel,
        out_shape=[jax.ShapeDtypeStruct((M, D), F32), jax.ShapeDtypeStruct((M, D), BF16)],
        grid=(M // tm,),
        in_specs=[pl.BlockSpec((tm, D), lambda i: (i, 0)),
                  pl.BlockSpec((D, D), lambda i: (0, 0)),
                  pl.BlockSpec((tm, D), lambda i: (i, 0)),
                  pl.BlockSpec((None, N_MOD, D), _mod_map(mod.shape[0], rows_per_group, tm)),
                  pl.BlockSpec((1, D), lambda i: (0, 0)),
                  pl.BlockSpec((1, D), lambda i: (0, 0))],
        out_specs=[pl.BlockSpec((tm, D), lambda i: (i, 0)),
                   pl.BlockSpec((tm, D), lambda i: (i, 0))],
        scratch_shapes=[pltpu.VMEM((tm, D), F32)],
        compiler_params=pltpu.CompilerParams(
            dimension_semantics=("parallel",),
            vmem_limit_bytes=_vmem_limit(blocks, _nbytes((tm, D), F32), _nbytes((tm, D), F32))),
        name="merge_b",
    )(merged, w_out, x2, mod, g_post, g_pre)


def _mlp_kernel(h_ref, wu_ref, wd_ref, x_ref, mod_ref, g_ref, o_ref, acc_ref):
    j = pl.program_id(1)

    @pl.when(j == 0)
    def _():
        acc_ref[...] = jnp.zeros_like(acc_ref)

    u = jnp.maximum(jnp.dot(h_ref[...], wu_ref[...], preferred_element_type=F32), 0.0)
    acc_ref[...] += jnp.dot((u * u).astype(BF16), wd_ref[...], preferred_element_type=F32)

    @pl.when(j == pl.num_programs(1) - 1)
    def _():
        gate = mod_ref[5:6, :]
        gpost = g_ref[...]
        n = x_ref.shape[0] // ROW_CHUNK

        def body(r, carry):
            sl = pl.ds(pl.multiple_of(r * ROW_CHUNK, ROW_CHUNK), ROW_CHUNK)
            y = acc_ref[sl, :]
            yn = y * lax.rsqrt(jnp.mean(y * y, axis=-1, keepdims=True) + EPS) * gpost
            o_ref[sl, :] = x_ref[sl, :] + gate * yn
            return carry

        lax.fori_loop(0, n, body, 0)


def _mlp(h2, w_up, w_down, x1, mod, g_post, *, rows_per_group, tm=512, tf=1024):
    M, D = x1.shape
    Fh = w_up.shape[1]
    tm = min(tm, M)
    blocks = (_nbytes((tm, D), BF16) + 2 * _nbytes((D, tf), BF16) + 2 * _nbytes((tm, D), F32)
              + _nbytes((N_MOD + 1, D), F32))
    return pl.pallas_call(
        _mlp_kernel,
        out_shape=jax.ShapeDtypeStruct((M, D), F32),
        grid=(M // tm, Fh // tf),
        in_specs=[pl.BlockSpec((tm, D), lambda i, j: (i, 0)),
                  pl.BlockSpec((D, tf), lambda i, j: (0, j)),
                  pl.BlockSpec((tf, D), lambda i, j: (j, 0)),
                  pl.BlockSpec((tm, D), lambda i, j: (i, 0)),
                  pl.BlockSpec((None, N_MOD, D), _mod_map(mod.shape[0], rows_per_group, tm)),
                  pl.BlockSpec((1, D), lambda i, j: (0, 0))],
        out_specs=pl.BlockSpec((tm, D), lambda i, j: (i, 0)),
        scratch_shapes=[pltpu.VMEM((tm, D), F32)],
        compiler_params=pltpu.CompilerParams(
            dimension_semantics=("parallel", "arbitrary"),
            vmem_limit_bytes=_vmem_limit(blocks, _nbytes((tm, D), F32),
                                         2 * _nbytes((tm, tf), F32) + _nbytes((tm, D), F32))),
        name="mlp",
    )(h2, w_up, w_down, x1, mod, g_post)


def _rope_tables(T, rope_half):
    rows = T // GRID_W
    row = jnp.repeat(jnp.arange(rows, dtype=F32), GRID_W)
    col = jnp.tile(jnp.arange(GRID_W, dtype=F32), rows)
    inv = ROPE_BASE ** (-jnp.arange(0, rope_half, 2, dtype=F32) / rope_half)
    ar, ac = row[:, None] * inv, col[:, None] * inv
    z = jnp.zeros_like(ar)
    cos = jnp.concatenate([jnp.cos(ar), jnp.cos(ar), jnp.cos(ac), jnp.cos(ac)], axis=-1)
    sin_next = jnp.concatenate([-jnp.sin(ar), z, -jnp.sin(ac), z], axis=-1)
    sin_prev = jnp.concatenate([z, jnp.sin(ar), z, jnp.sin(ac)], axis=-1)
    return cos, sin_next, sin_prev


def kernel(x_prompt, x_sample, cache_k, cache_v, state_ret_fwd, state_ret_bwd, c, c_ctx, w_ada, b_ada, g_mix_pre, g_mix_post, g_mlp_pre, g_mlp_post, w_in, ret_gamma_logit_fwd, ret_gamma_logit_bwd, w_ret_o, lambda_q1, lambda_k1, lambda_q2, lambda_k2, g_diff_subln, w_diff_o, w_out, w_mlp_up, w_mlp_down):
    Bp, Tp, D = x_prompt.shape
    Bs, Ts, _ = x_sample.shape
    depth = w_in.shape[0]
    assert depth == 1
    l = 0
    H = N_RET_HEADS
    dk = D // H
    dv = 2 * dk
    dh = D // N_DIFF_HEADS // 2
    ret_qk_w, ret_v_w, diff_w = H * dk, H * dv, N_DIFF_HEADS * 2 * dh
    o_dq = 2 * ret_qk_w + 2 * ret_v_w
    o_dk, o_dv, o_gate = o_dq + diff_w, o_dq + 2 * diff_w, o_dq + 3 * diff_w
    in_w = o_gate + 2 * D
    assert w_in.shape[2] == in_w
    lam_init = 0.8 - 0.6 * math.exp(-0.3 * l)
    q_mult = float(dh) ** -0.5 * math.log2(math.e)

    w_in_b = w_in[l].astype(BF16)
    w_a = w_in_b[:, :o_dq]
    w_dq = w_in_b[:, o_dq:o_dk]
    w_dk = w_in_b[:, o_dk:o_dv]
    w_dqk = w_in_b[:, o_dq:o_dv]
    w_dvg = w_in_b[:, o_dv:]
    w_dv = w_in_b[:, o_dv:o_gate]
    w_g = w_in_b[:, o_gate:]
    w_ret_o_b = w_ret_o[l].astype(BF16)
    w_diff_o_b = w_diff_o[l].astype(BF16)
    w_out_b = w_out[l].astype(BF16)
    w_up_b = w_mlp_up[l].astype(BF16)
    w_down_b = w_mlp_down[l].astype(BF16)
    g_mix_pre_l = g_mix_pre[l].reshape(1, D)
    g_mix_post_l = g_mix_post[l].reshape(1, D)
    g_mlp_pre_l = g_mlp_pre[l].reshape(1, D)
    g_mlp_post_l = g_mlp_post[l].reshape(1, D)
    g_subln_l = g_diff_subln[l].reshape(1, 2 * dh)
    lam_params = jnp.concatenate(
        [jnp.stack([lambda_q1[l], lambda_k1[l], lambda_q2[l], lambda_k2[l]]), jnp.zeros((4, dh), F32)], axis=0)
    lg = jnp.stack([ret_gamma_logit_fwd[l], ret_gamma_logit_bwd[l]], axis=1)
    lg = jnp.concatenate([lg, jnp.zeros((H, 6), F32)], axis=1)
    lg = jnp.broadcast_to(lg[:, :, None], (H, 8, V7X_LANES))

    n_cond = 16
    cond = jnp.concatenate([c, jnp.broadcast_to(c_ctx[None, :], (n_cond - Bs, D))], axis=0)
    mod = _adaln(cond, w_ada[l], b_ada[l].reshape(1, N_MOD * D)).reshape(n_cond, N_MOD, D)

    ret_offs = dict(off_q=0, off_k=ret_qk_w, off_v=2 * ret_qk_w, off_g=2 * ret_qk_w + ret_v_w)

    def tail(x2, B, T, rows_per_group, ret_g, diff_n, gates, off_gr):
        merged = _merge_a(ret_g.reshape(B * T, ret_v_w), diff_n.reshape(B * T, diff_w), w_ret_o_b, w_diff_o_b,
                          gates, off_gr=off_gr, off_gd=off_gr + D)
        x1, h2 = _merge_b(merged, w_out_b, x2, mod, g_mix_post_l, g_mlp_pre_l, rows_per_group=rows_per_group)
        y = _mlp(h2, w_up_b, w_down_b, x1, mod, g_mlp_post_l, rows_per_group=rows_per_group)
        return y.reshape(B, T, D)

    xp2 = x_prompt.reshape(Bp * Tp, D)
    hp = _prenorm(xp2, mod, g_mix_pre_l, rows_per_group=None)
    pa = _proj(hp, w_a, out_dtype=BF16, name="proj_ret").reshape(Bp, Tp, -1)
    pq = _proj(hp, w_dq, out_dtype=BF16, n_q_cols=diff_w, q_mult=q_mult, name="proj_q").reshape(Bp, Tp, diff_w)
    pk = _proj(hp, w_dk, out_dtype=F32, name="proj_k")
    pv = _proj(hp, w_dv, out_dtype=F32, name="proj_v")
    pg = _proj(hp, w_g, out_dtype=BF16, name="proj_gates")
    ret_g, new_state_fwd, new_state_bwd = _retention(pa, lg, None, emit_state=True, dk=dk, dv=dv, **ret_offs)
    diff_n = _diffattn((pq, 0), (pk.reshape(Bp, Tp, diff_w), 0), (pv.reshape(Bp, Tp, diff_w), 0),
                       lam_params, g_subln_l, None, lam_init=lam_init, dh=dh)
    y_prompt = tail(xp2, Bp, Tp, None, ret_g, diff_n, pg, 0)
    new_cache_k = pk.reshape(Bp, 1, Tp, N_DIFF_HEADS, 2, dh)
    new_cache_v = pv.reshape(Bp, 1, Tp, N_DIFF_HEADS, 2 * dh)

    xs2 = x_sample.reshape(Bs * Ts, D)
    hs = _prenorm(xs2, mod, g_mix_pre_l, rows_per_group=Ts)
    sa = _proj(hs, w_a, out_dtype=BF16, tm=2048, name="proj_ret").reshape(Bs, Ts, -1)
    sqk = _proj(hs, w_dqk, out_dtype=BF16, n_q_cols=diff_w, q_mult=q_mult,
                rope_tables=_rope_tables(Ts, dh // 2), name="proj_qk_rope").reshape(Bs, Ts, 2 * diff_w)
    svg = _proj(hs, w_dvg, out_dtype=BF16, tm=2048, name="proj_vg")
    (ret_g,) = _retention(sa, lg, (state_ret_fwd, state_ret_bwd), emit_state=False, dk=dk, dv=dv, **ret_offs)
    caches = (cache_k[:, l].reshape(Bs, -1, diff_w), cache_v[:, l].reshape(Bs, -1, diff_w))
    diff_n = _diffattn((sqk, 0), (sqk, diff_w), (svg.reshape(Bs, Ts, -1), 0),
                       lam_params, g_subln_l, caches, lam_init=lam_init, dh=dh)
    y_sample = tail(xs2, Bs, Ts, Ts, ret_g, diff_n, svg, diff_w)

    return (y_prompt, y_sample, new_cache_k, new_cache_v, new_state_fwd, new_state_bwd)
```

```python
import functools
import math

import jax
import jax.numpy as jnp
from jax import lax
from jax.experimental import pallas as pl
from jax.experimental.pallas import tpu as pltpu

F32 = jnp.float32
BF16 = jnp.bfloat16

N_RET_HEADS = 8
N_DIFF_HEADS = 8
N_MOD = 6
GRID_W = 64
ROPE_BASE = 10000.0
EPS = 1e-6
SUBLN_EPS = 1e-5

V7X_LANES = 128
V7X_VMEM_BYTES = 64 * 1024 * 1024
VMEM_RESERVE_BYTES = 8 * 1024 * 1024


def _vmem_limit(block_bytes, scratch_bytes=0, temp_bytes=0):
    want = 2 * block_bytes + scratch_bytes + temp_bytes + VMEM_RESERVE_BYTES
    return int(min(want, V7X_VMEM_BYTES - VMEM_RESERVE_BYTES // 2))


def _nbytes(shape, dtype):
    return math.prod(shape) * jnp.dtype(dtype).itemsize


def _mod_map(n_groups, rows_per_group, tm):
    if rows_per_group is None:
        return lambda i, *_: (n_groups - 1, 0, 0)
    return lambda i, *_: ((i * tm) // rows_per_group, 0, 0)


def _adaln_kernel(c_ref, w_ref, b_ref, o_ref):
    c = c_ref[...]
    s = (c * jax.nn.sigmoid(c)).astype(BF16)
    o_ref[...] = jnp.dot(s, w_ref[...].astype(BF16), preferred_element_type=F32) + b_ref[...]


def _adaln(cond, w_ada, b_ada, tn=512):
    R, D = cond.shape
    N = w_ada.shape[1]
    blocks = _nbytes((R, D), F32) + _nbytes((D, tn), F32) + _nbytes((1, tn), F32) + _nbytes((R, tn), F32)
    return pl.pallas_call(
        _adaln_kernel,
        out_shape=jax.ShapeDtypeStruct((R, N), F32),
        grid=(N // tn,),
        in_specs=[pl.BlockSpec((R, D), lambda j: (0, 0)),
                  pl.BlockSpec((D, tn), lambda j: (0, j)),
                  pl.BlockSpec((1, tn), lambda j: (0, j))],
        out_specs=pl.BlockSpec((R, tn), lambda j: (0, j)),
        compiler_params=pltpu.CompilerParams(
            dimension_semantics=("arbitrary",),
            vmem_limit_bytes=_vmem_limit(blocks, temp_bytes=_nbytes((D, tn), BF16))),
        name="adaln",
    )(cond, w_ada, b_ada)


ROW_CHUNK = 64


def _norm_modulate(x_ref, gain, shift, scale, h_ref):
    a = gain * (1.0 + scale)
    n = x_ref.shape[0] // ROW_CHUNK

    def body(r, carry):
        sl = pl.ds(pl.multiple_of(r * ROW_CHUNK, ROW_CHUNK), ROW_CHUNK)
        x = x_ref[sl, :]
        ms = jnp.mean(x * x, axis=-1, keepdims=True)
        h_ref[sl, :] = (x * lax.rsqrt(ms + EPS) * a + shift).astype(h_ref.dtype)
        return carry

    lax.fori_loop(0, n, body, 0)


def _prenorm_kernel(x_ref, mod_ref, g_ref, h_ref):
    _norm_modulate(x_ref, g_ref[...], mod_ref[0:1, :], mod_ref[1:2, :], h_ref)


def _prenorm(x2, mod, gain, *, rows_per_group, tm=512):
    M, D = x2.shape
    blocks = _nbytes((tm, D), F32) + _nbytes((tm, D), BF16) + _nbytes((N_MOD + 1, D), F32)
    return pl.pallas_call(
        _prenorm_kernel,
        out_shape=jax.ShapeDtypeStruct((M, D), BF16),
        grid=(M // tm,),
        in_specs=[pl.BlockSpec((tm, D), lambda i: (i, 0)),
                  pl.BlockSpec((None, N_MOD, D), _mod_map(mod.shape[0], rows_per_group, tm)),
                  pl.BlockSpec((1, D), lambda i: (0, 0))],
        out_specs=pl.BlockSpec((tm, D), lambda i: (i, 0)),
        compiler_params=pltpu.CompilerParams(
            dimension_semantics=("parallel",),
            vmem_limit_bytes=_vmem_limit(blocks, temp_bytes=4 * _nbytes((ROW_CHUNK, D), F32))),
        name="prenorm",
    )(x2, mod, gain)


CAST_ROWS = 256


def _proj_kernel(h_ref, w_ref, *rest, n_q_blocks, q_mult, rope):
    if rope:
        cos_ref, sa_ref, sb_ref, o_ref, wb_ref = rest
    else:
        o_ref, wb_ref = rest

    @pl.when(pl.program_id(1) == 0)
    def _():
        def body(r, carry):
            sl = pl.ds(pl.multiple_of(r * CAST_ROWS, CAST_ROWS), CAST_ROWS)
            wb_ref[sl, :] = w_ref[sl, :].astype(wb_ref.dtype)
            return carry

        lax.fori_loop(0, w_ref.shape[0] // CAST_ROWS, body, 0)

    acc = jnp.dot(h_ref[...], wb_ref[...], preferred_element_type=F32)
    mult = None
    if n_q_blocks:
        mult = jnp.where(pl.program_id(0) < n_q_blocks, q_mult, 1.0).astype(F32)
    if rope:
        cos, sa, sb = cos_ref[...] * mult, sa_ref[...] * mult, sb_ref[...] * mult
        for c in range(acc.shape[1] // V7X_LANES):
            cs = slice(c * V7X_LANES, (c + 1) * V7X_LANES)
            xc = acc[:, cs]
            y = xc * cos + pltpu.roll(xc, 96, 1) * sa + pltpu.roll(xc, 32, 1) * sb
            o_ref[:, cs] = y.astype(o_ref.dtype)
    elif mult is not None:
        o_ref[...] = (acc * mult).astype(o_ref.dtype)
    else:
        o_ref[...] = acc.astype(o_ref.dtype)


def _proj(h, w, cols, *, out_dtype, n_q_cols=0, q_mult=1.0, rope_tables=None, tm=1024, tn=1024, name="proj"):
    M, D = h.shape
    N = cols[1] - cols[0]
    tm = min(tm, M)
    tn = min(tn, N)
    j0 = cols[0] // tn
    in_specs = [pl.BlockSpec((tm, D), lambda j, i: (i, 0)),
                pl.BlockSpec((D, tn), lambda j, i: (0, j0 + j))]
    args = [h, w]
    blocks = _nbytes((tm, D), BF16) + _nbytes((D, tn), w.dtype) + _nbytes((tm, tn), out_dtype)
    if rope_tables is not None:
        nt = rope_tables[0].shape[0] // tm
        for t in rope_tables:
            in_specs.append(pl.BlockSpec((tm, V7X_LANES), lambda j, i: (i % nt, 0)))
            args.append(t)
        blocks += 3 * _nbytes((tm, V7X_LANES), F32)
    return pl.pallas_call(
        functools.partial(_proj_kernel, n_q_blocks=n_q_cols // tn, q_mult=q_mult, rope=rope_tables is not None),
        out_shape=jax.ShapeDtypeStruct((M, N), out_dtype),
        grid=(N // tn, M // tm),
        in_specs=in_specs,
        out_specs=pl.BlockSpec((tm, tn), lambda j, i: (i, j)),
        scratch_shapes=[pltpu.VMEM((D, tn), BF16)],
        compiler_params=pltpu.CompilerParams(
            dimension_semantics=("parallel", "arbitrary"),
            vmem_limit_bytes=_vmem_limit(blocks, _nbytes((D, tn), BF16), _nbytes((tm, tn), F32))),
        name=name,
    )(*args)


RET_CHUNK = 256


def _log_sigmoid(x):
    return jnp.minimum(x, 0.0) - jnp.log1p(jnp.exp(-jnp.abs(x)))


def _ret_kernel(lg_ref, q_ref, k_ref, v_ref, g_ref, *rest, has_state, emit_state, k_scale):
    rest = list(rest)
    s0f_ref = s0b_ref = sf_out = sb_out = None
    if has_state:
        s0f_ref, s0b_ref = rest[:2]
        rest = rest[2:]
    o_ref = rest[0]
    rest = rest[1:]
    if emit_state:
        sf_out, sb_out = rest[:2]
        rest = rest[2:]
    sf_scr, sb_scr, st_scr = rest

    T, dk = q_ref.shape
    C = min(RET_CHUNK, T)
    n = T // C

    lsig = _log_sigmoid(lg_ref[...])
    lgf = lsig[0:1, 0:1]
    lgb = lsig[1:2, 0:1]
    ri = lax.broadcasted_iota(jnp.int32, (C, C), 0)
    ci = lax.broadcasted_iota(jnp.int32, (C, C), 1)
    rel = (ri - ci).astype(F32)
    decay = jnp.where(rel > 0, jnp.exp(rel * lgf), jnp.where(rel < 0, jnp.exp(-rel * lgb), 2.0)) * k_scale
    li = lax.broadcasted_iota(jnp.int32, (C, 1), 0).astype(F32)
    xi_f = jnp.exp((li + 1.0) * lgf)
    xi_b = jnp.exp((C - li) * lgb)
    zeta_f = jnp.exp((C - 1.0 - li) * lgf) * k_scale
    zeta_b = jnp.exp(li * lgb) * k_scale
    gc_f = jnp.exp(C * lgf)
    gc_b = jnp.exp(C * lgb)

    if has_state:
        sf_scr[...] = s0f_ref[...]
        sb_scr[...] = s0b_ref[...]
    else:
        sf_scr[...] = jnp.zeros_like(sf_scr)
        sb_scr[...] = jnp.zeros_like(sb_scr)

    def rows(c):
        return slice(c * C, (c + 1) * C)

    def kv_outer(c, zeta):
        ks = (k_ref[rows(c), :].astype(F32) * zeta).astype(BF16)
        return lax.dot_general(ks, v_ref[rows(c), :], (((0,), (0,)), ((), ())), preferred_element_type=F32)

    n_upd = n if emit_state else n - 1
    kv_next = (kv_outer(0, zeta_f), kv_outer(n - 1, zeta_b)) if n_upd > 0 else None
    for t in range(n):
        cf, cb = t, n - 1 - t
        st_scr[cf, 0:dk, :] = sf_scr[...].astype(BF16)
        st_scr[cb, dk:2 * dk, :] = sb_scr[...].astype(BF16)
        if t < n_upd:
            kvf, kvb = kv_next
            if t + 1 < n_upd:
                kv_next = (kv_outer(cf + 1, zeta_f), kv_outer(cb - 1, zeta_b))
            sf_scr[...] = gc_f * sf_scr[...] + kvf
            sb_scr[...] = gc_b * sb_scr[...] + kvb

    if emit_state:
        sf_out[...] = sf_scr[...]
        sb_out[...] = sb_scr[...]

    def chunk_out(c):
        qc = q_ref[rows(c), :]
        s = lax.dot_general(qc, k_ref[rows(c), :], (((1,), (1,)), ((), ())), preferred_element_type=F32) * decay
        qf = qc.astype(F32)
        qx = jnp.concatenate([(qf * xi_f).astype(BF16), (qf * xi_b).astype(BF16)], axis=1)
        return (jnp.dot(s.astype(BF16), v_ref[rows(c), :], preferred_element_type=F32)
                + jnp.dot(qx, st_scr[c], preferred_element_type=F32))

    o_next = chunk_out(0)
    for c in range(n):
        o = o_next
        if c + 1 < n:
            o_next = chunk_out(c + 1)
        on = o * lax.rsqrt(jnp.mean(o * o, axis=-1, keepdims=True) + EPS)
        g = g_ref[rows(c), :].astype(F32)
        o_ref[rows(c), :] = (g * jax.nn.sigmoid(g) * on).astype(o_ref.dtype)


def _retention(pa3, lg, states, *, emit_state, dk, dv, off_q, off_k, off_v, off_g):
    B, T, _ = pa3.shape
    H = N_RET_HEADS
    has_state = states is not None
    C = min(RET_CHUNK, T)
    n = T // C
    in_specs = [pl.BlockSpec((None, 8, V7X_LANES), lambda b, h: (h, 0, 0)),
                pl.BlockSpec((None, T, dk), lambda b, h: (b, 0, off_q // dk + h)),
                pl.BlockSpec((None, T, dk), lambda b, h: (b, 0, off_k // dk + h)),
                pl.BlockSpec((None, T, dv), lambda b, h: (b, 0, off_v // dv + h)),
                pl.BlockSpec((None, T, dv), lambda b, h: (b, 0, off_g // dv + h))]
    args = [lg, pa3, pa3, pa3, pa3]
    blocks = 2 * _nbytes((T, dk), BF16) + 3 * _nbytes((T, dv), BF16) + _nbytes((8, V7X_LANES), F32)
    state_spec = pl.BlockSpec((None, None, None, dk, dv), lambda b, h: (b, 0, h, 0, 0))
    if has_state:
        in_specs += [state_spec, state_spec]
        args += list(states)
        blocks += 2 * _nbytes((dk, dv), F32)
    out_shape = [jax.ShapeDtypeStruct((B, T, H * dv), BF16)]
    out_specs = [pl.BlockSpec((None, T, dv), lambda b, h: (b, 0, h))]
    if emit_state:
        out_shape += [jax.ShapeDtypeStruct((B, 1, H, dk, dv), F32)] * 2
        out_specs += [state_spec, state_spec]
        blocks += 2 * _nbytes((dk, dv), F32)
    scratch = [pltpu.VMEM((dk, dv), F32), pltpu.VMEM((dk, dv), F32), pltpu.VMEM((n, 2 * dk, dv), BF16)]
    scratch_bytes = 2 * _nbytes((dk, dv), F32) + _nbytes((n, 2 * dk, dv), BF16)
    temps = 8 * _nbytes((C, dv), F32) + 6 * _nbytes((dk, dv), F32) + 4 * _nbytes((C, C), F32)
    return pl.pallas_call(
        functools.partial(_ret_kernel, has_state=has_state, emit_state=emit_state, k_scale=float(dk) ** -0.5),
        out_shape=out_shape,
        grid=(B, H),
        in_specs=in_specs,
        out_specs=out_specs,
        scratch_shapes=scratch,
        compiler_params=pltpu.CompilerParams(
            dimension_semantics=("parallel", "parallel"),
            vmem_limit_bytes=_vmem_limit(blocks, scratch_bytes, temps)),
        name="retention",
    )(*args)


DIFF_SUB_ROWS = 256


def _diff_kernel(lam_ref, gs_ref, q_ref, k_ref, v_ref, *rest, has_cache, lam_init, dh, sub):
    if has_cache:
        ck_ref, cv_ref, o_ref, kall, vall = rest
    else:
        o_ref, kall, vall = rest
    T = k_ref.shape[0]

    @pl.when(pl.program_id(2) == 0)
    def _():
        kall[0:T, :] = k_ref[...].astype(kall.dtype)
        vall[0:T, :] = v_ref[...].astype(vall.dtype)
        if has_cache:
            kall[T:, :] = ck_ref[...].astype(kall.dtype)
            vall[T:, :] = cv_ref[...].astype(vall.dtype)

    lp = lam_ref[...]
    lam = (jnp.exp(jnp.sum(lp[0:1, :] * lp[1:2, :], axis=-1, keepdims=True))
           - jnp.exp(jnp.sum(lp[2:3, :] * lp[3:4, :], axis=-1, keepdims=True)) + lam_init)
    gain = gs_ref[...] * (1.0 - lam_init)
    n_sub = q_ref.shape[0] // sub

    def scores(r):
        q = q_ref[r * sub:(r + 1) * sub, :]
        return [lax.dot_general(q[:, m * dh:(m + 1) * dh], kall[:, m * dh:(m + 1) * dh],
                                (((1,), (1,)), ((), ())), preferred_element_type=F32) for m in range(2)]

    s_next = scores(0)
    for r in range(n_sub):
        s_cur = s_next
        if r + 1 < n_sub:
            s_next = scores(r + 1)
        probs = []
        for s in s_cur:
            e = jnp.exp2(s - jnp.max(s, axis=-1, keepdims=True))
            probs.append((e, jnp.sum(e, axis=-1, keepdims=True)))
        (e1, l1), (e2, l2) = probs
        r1 = 1.0 / l1
        a = (e1 - e2 * (lam * l1 / l2)).astype(BF16)
        o = jnp.dot(a, vall[...], preferred_element_type=F32) * r1
        on = o * lax.rsqrt(jnp.mean(o * o, axis=-1, keepdims=True) + SUBLN_EPS)
        o_ref[r * sub:(r + 1) * sub, :] = (on * gain).astype(o_ref.dtype)


def _diffattn(q_src, k_src, v_src, lam_params, g_subln, caches, *, lam_init, dh, tq=2048):
    (qa, off_q), (ka, off_k), (va, off_v) = q_src, k_src, v_src
    B, T, _ = qa.shape
    H = N_DIFF_HEADS
    w = 2 * dh
    tq = min(tq, T)
    sub = min(DIFF_SUB_ROWS, tq)
    has_cache = caches is not None
    Tk = T + (caches[0].shape[1] if has_cache else 0)
    in_specs = [pl.BlockSpec((8, dh), lambda b, h, i: (0, 0)),
                pl.BlockSpec((1, w), lambda b, h, i: (0, 0)),
                pl.BlockSpec((None, tq, w), lambda b, h, i: (b, i, off_q // w + h)),
                pl.BlockSpec((None, T, w), lambda b, h, i: (b, 0, off_k // w + h)),
                pl.BlockSpec((None, T, w), lambda b, h, i: (b, 0, off_v // w + h))]
    args = [lam_params, g_subln, qa, ka, va]
    blocks = 2 * _nbytes((tq, w), BF16) + _nbytes((T, w), ka.dtype) + _nbytes((T, w), va.dtype)
    if has_cache:
        P = caches[0].shape[1]
        cspec = pl.BlockSpec((None, P, w), lambda b, h, i: (b, 0, h))
        in_specs += [cspec, cspec]
        args += list(caches)
        blocks += 2 * _nbytes((P, w), F32)
    temps = 10 * _nbytes((sub, Tk), F32)
    return pl.pallas_call(
        functools.partial(_diff_kernel, has_cache=has_cache, lam_init=lam_init, dh=dh, sub=sub),
        out_shape=jax.ShapeDtypeStruct((B, T, H * w), BF16),
        grid=(B, H, T // tq),
        in_specs=in_specs,
        out_specs=pl.BlockSpec((None, tq, w), lambda b, h, i: (b, i, h)),
        scratch_shapes=[pltpu.VMEM((Tk, w), BF16), pltpu.VMEM((Tk, w), BF16)],
        compiler_params=pltpu.CompilerParams(
            dimension_semantics=("parallel", "parallel", "arbitrary"),
            vmem_limit_bytes=_vmem_limit(blocks, 2 * _nbytes((Tk, w), BF16), temps)),
        name="diffattn",
    )(*args)


def _merge_a_kernel(r_ref, d_ref, wr_ref, wd_ref, gr_ref, gd_ref, o_ref):
    rb = jnp.dot(r_ref[...], wr_ref[...], preferred_element_type=F32)
    db = jnp.dot(d_ref[...], wd_ref[...], preferred_element_type=F32)
    gr = jax.nn.sigmoid(gr_ref[...].astype(F32))
    gd = jax.nn.sigmoid(gd_ref[...].astype(F32))
    o_ref[...] = (gr * rb + gd * db).astype(o_ref.dtype)


def _merge_a(ret_g, diff_n, w_ret_o, w_diff_o, gates, *, off_gr, off_gd, tm=1024, tn=512):
    M, Kr = ret_g.shape
    Kd = diff_n.shape[1]
    D = w_ret_o.shape[1]
    tm = min(tm, M)
    blocks = (_nbytes((tm, Kr), BF16) + _nbytes((tm, Kd), BF16) + _nbytes((Kr, tn), BF16)
              + _nbytes((Kd, tn), BF16) + 3 * _nbytes((tm, tn), BF16))
    return pl.pallas_call(
        _merge_a_kernel,
        out_shape=jax.ShapeDtypeStruct((M, D), BF16),
        grid=(M // tm, D // tn),
        in_specs=[pl.BlockSpec((tm, Kr), lambda i, j: (i, 0)),
                  pl.BlockSpec((tm, Kd), lambda i, j: (i, 0)),
                  pl.BlockSpec((Kr, tn), lambda i, j: (0, j)),
                  pl.BlockSpec((Kd, tn), lambda i, j: (0, j)),
                  pl.BlockSpec((tm, tn), lambda i, j: (i, off_gr // tn + j)),
                  pl.BlockSpec((tm, tn), lambda i, j: (i, off_gd // tn + j))],
        out_specs=pl.BlockSpec((tm, tn), lambda i, j: (i, j)),
        compiler_params=pltpu.CompilerParams(
            dimension_semantics=("parallel", "arbitrary"),
            vmem_limit_bytes=_vmem_limit(blocks, temp_bytes=4 * _nbytes((tm, tn), F32))),
        name="merge_a",
    )(ret_g, diff_n, w_ret_o, w_diff_o, gates, gates)


MERGE_SUB_ROWS = 256


def _merge_b_kernel(m_ref, w_ref, x_ref, mod_ref, gpost_ref, gpre_ref, x1_ref, h2_ref):
    gate_post = mod_ref[2:3, :] * gpost_ref[...]
    a2 = gpre_ref[...] * (1.0 + mod_ref[4:5, :])
    shift2 = mod_ref[3:4, :]
    sub = MERGE_SUB_ROWS
    n_sub = x_ref.shape[0] // sub

    def out_proj(k):
        return jnp.dot(m_ref[k * sub:(k + 1) * sub, :], w_ref[...], preferred_element_type=F32)

    y_next = out_proj(0)
    for k in range(n_sub):
        y_all = y_next
        if k + 1 < n_sub:
            y_next = out_proj(k + 1)
        for c in range(sub // ROW_CHUNK):
            sl = slice(k * sub + c * ROW_CHUNK, k * sub + (c + 1) * ROW_CHUNK)
            y = y_all[c * ROW_CHUNK:(c + 1) * ROW_CHUNK, :]
            x1 = x_ref[sl, :] + y * lax.rsqrt(jnp.mean(y * y, axis=-1, keepdims=True) + EPS) * gate_post
            x1_ref[sl, :] = x1
            h2_ref[sl, :] = (x1 * lax.rsqrt(jnp.mean(x1 * x1, axis=-1, keepdims=True) + EPS) * a2
                             + shift2).astype(h2_ref.dtype)


def _merge_b(merged, w_out, x2, mod, g_post, g_pre, *, rows_per_group, tm=512):
    M, D = x2.shape
    tm = min(tm, M)
    blocks = (_nbytes((tm, D), BF16) * 2 + _nbytes((D, D), BF16) + 2 * _nbytes((tm, D), F32)
              + _nbytes((N_MOD + 2, D), F32))
    return pl.pallas_call(
        _merge_b_kernel,
        out_shape=[jax.ShapeDtypeStruct((M, D), F32), jax.ShapeDtypeStruct((M, D), BF16)],
        grid=(M // tm,),
        in_specs=[pl.BlockSpec((tm, D), lambda i: (i, 0)),
                  pl.BlockSpec((D, D), lambda i: (0, 0)),
                  pl.BlockSpec((tm, D), lambda i: (i, 0)),
                  pl.BlockSpec((None, N_MOD, D), _mod_map(mod.shape[0], rows_per_group, tm)),
                  pl.BlockSpec((1, D), lambda i: (0, 0)),
                  pl.BlockSpec((1, D), lambda i: (0, 0))],
        out_specs=[pl.BlockSpec((tm, D), lambda i: (i, 0)),
                   pl.BlockSpec((tm, D), lambda i: (i, 0))],
        compiler_params=pltpu.CompilerParams(
            dimension_semantics=("parallel",),
            vmem_limit_bytes=_vmem_limit(blocks, temp_bytes=2 * _nbytes((tm, D), F32))),
        name="merge_b",
    )(merged, w_out, x2, mod, g_post, g_pre)


def _mlp_kernel(h_ref, wu_ref, wd_ref, x_ref, mod_ref, g_ref, o_ref, acc_ref):
    j = pl.program_id(1)

    @pl.when(j == 0)
    def _():
        acc_ref[...] = jnp.zeros_like(acc_ref)

    u = jnp.maximum(jnp.dot(h_ref[...], wu_ref[...], preferred_element_type=F32), 0.0)
    acc_ref[...] += jnp.dot((u * u).astype(BF16), wd_ref[...], preferred_element_type=F32)

    @pl.when(j == pl.num_programs(1) - 1)
    def _():
        gate = mod_ref[5:6, :]
        gpost = g_ref[...]
        n = x_ref.shape[0] // ROW_CHUNK

        def body(r, carry):
            sl = pl.ds(pl.multiple_of(r * ROW_CHUNK, ROW_CHUNK), ROW_CHUNK)
            y = acc_ref[sl, :]
            yn = y * lax.rsqrt(jnp.mean(y * y, axis=-1, keepdims=True) + EPS) * gpost
            o_ref[sl, :] = x_ref[sl, :] + gate * yn
            return carry

        lax.fori_loop(0, n, body, 0)


def _mlp(h2, w_up, w_down, x1, mod, g_post, *, rows_per_group, tm=512, tf=1024):
    M, D = x1.shape
    Fh = w_up.shape[1]
    tm = min(tm, M)
    blocks = (_nbytes((tm, D), BF16) + 2 * _nbytes((D, tf), BF16) + 2 * _nbytes((tm, D), F32)
              + _nbytes((N_MOD + 1, D), F32))
    return pl.pallas_call(
        _mlp_kernel,
        out_shape=jax.ShapeDtypeStruct((M, D), F32),
        grid=(M // tm, Fh // tf),
        in_specs=[pl.BlockSpec((tm, D), lambda i, j: (i, 0)),
                  pl.BlockSpec((D, tf), lambda i, j: (0, j)),
                  pl.BlockSpec((tf, D), lambda i, j: (j, 0)),
                  pl.BlockSpec((tm, D), lambda i, j: (i, 0)),
                  pl.BlockSpec((None, N_MOD, D), _mod_map(mod.shape[0], rows_per_group, tm)),
                  pl.BlockSpec((1, D), lambda i, j: (0, 0))],
        out_specs=pl.BlockSpec((tm, D), lambda i, j: (i, 0)),
        scratch_shapes=[pltpu.VMEM((tm, D), F32)],
        compiler_params=pltpu.CompilerParams(
            dimension_semantics=("parallel", "arbitrary"),
            vmem_limit_bytes=_vmem_limit(blocks, _nbytes((tm, D), F32),
                                         2 * _nbytes((tm, tf), F32) + _nbytes((tm, D), F32))),
        name="mlp",
    )(h2, w_up, w_down, x1, mod, g_post)


def _rope_tables(T, rope_half):
    rows = T // GRID_W
    row = jnp.repeat(jnp.arange(rows, dtype=F32), GRID_W)
    col = jnp.tile(jnp.arange(GRID_W, dtype=F32), rows)
    inv = ROPE_BASE ** (-jnp.arange(0, rope_half, 2, dtype=F32) / rope_half)
    ar, ac = row[:, None] * inv, col[:, None] * inv
    z = jnp.zeros_like(ar)
    cos = jnp.concatenate([jnp.cos(ar), jnp.cos(ar), jnp.cos(ac), jnp.cos(ac)], axis=-1)
    sin_next = jnp.concatenate([-jnp.sin(ar), z, -jnp.sin(ac), z], axis=-1)
    sin_prev = jnp.concatenate([z, jnp.sin(ar), z, jnp.sin(ac)], axis=-1)
    return cos, sin_next, sin_prev


def kernel(x_prompt, x_sample, cache_k, cache_v, state_ret_fwd, state_ret_bwd, c, c_ctx, w_ada, b_ada, g_mix_pre, g_mix_post, g_mlp_pre, g_mlp_post, w_in, ret_gamma_logit_fwd, ret_gamma_logit_bwd, w_ret_o, lambda_q1, lambda_k1, lambda_q2, lambda_k2, g_diff_subln, w_diff_o, w_out, w_mlp_up, w_mlp_down):
    Bp, Tp, D = x_prompt.shape
    Bs, Ts, _ = x_sample.shape
    depth = w_in.shape[0]
    assert depth == 1
    l = 0
    H = N_RET_HEADS
    dk = D // H
    dv = 2 * dk
    dh = D // N_DIFF_HEADS // 2
    ret_qk_w, ret_v_w, diff_w = H * dk, H * dv, N_DIFF_HEADS * 2 * dh
    o_dq = 2 * ret_qk_w + 2 * ret_v_w
    o_dk, o_dv, o_gate = o_dq + diff_w, o_dq + 2 * diff_w, o_dq + 3 * diff_w
    in_w = o_gate + 2 * D
    assert w_in.shape[2] == in_w
    lam_init = 0.8 - 0.6 * math.exp(-0.3 * l)
    q_mult = float(dh) ** -0.5 * math.log2(math.e)

    w_in_l = w_in[l]
    w_ret_o_b = w_ret_o[l].astype(BF16)
    w_diff_o_b = w_diff_o[l].astype(BF16)
    w_out_b = w_out[l].astype(BF16)
    w_up_b = w_mlp_up[l].astype(BF16)
    w_down_b = w_mlp_down[l].astype(BF16)
    g_mix_pre_l = g_mix_pre[l].reshape(1, D)
    g_mix_post_l = g_mix_post[l].reshape(1, D)
    g_mlp_pre_l = g_mlp_pre[l].reshape(1, D)
    g_mlp_post_l = g_mlp_post[l].reshape(1, D)
    g_subln_l = g_diff_subln[l].reshape(1, 2 * dh)
    lam_params = jnp.concatenate(
        [jnp.stack([lambda_q1[l], lambda_k1[l], lambda_q2[l], lambda_k2[l]]), jnp.zeros((4, dh), F32)], axis=0)
    lg = jnp.stack([ret_gamma_logit_fwd[l], ret_gamma_logit_bwd[l]], axis=1)
    lg = jnp.concatenate([lg, jnp.zeros((H, 6), F32)], axis=1)
    lg = jnp.broadcast_to(lg[:, :, None], (H, 8, V7X_LANES))

    n_cond = 16
    cond = jnp.concatenate([c, jnp.broadcast_to(c_ctx[None, :], (n_cond - Bs, D))], axis=0)
    mod = _adaln(cond, w_ada[l], b_ada[l].reshape(1, N_MOD * D)).reshape(n_cond, N_MOD, D)

    ret_offs = dict(off_q=0, off_k=ret_qk_w, off_v=2 * ret_qk_w, off_g=2 * ret_qk_w + ret_v_w)

    def tail(x2, B, T, rows_per_group, ret_g, diff_n, gates, off_gr):
        merged = _merge_a(ret_g.reshape(B * T, ret_v_w), diff_n.reshape(B * T, diff_w), w_ret_o_b, w_diff_o_b,
                          gates, off_gr=off_gr, off_gd=off_gr + D)
        x1, h2 = _merge_b(merged, w_out_b, x2, mod, g_mix_post_l, g_mlp_pre_l, rows_per_group=rows_per_group)
        y = _mlp(h2, w_up_b, w_down_b, x1, mod, g_mlp_post_l, rows_per_group=rows_per_group)
        return y.reshape(B, T, D)

    xp2 = x_prompt.reshape(Bp * Tp, D)
    hp = _prenorm(xp2, mod, g_mix_pre_l, rows_per_group=None)
    pa = _proj(hp, w_in_l, (0, o_dq), out_dtype=BF16, name="proj_ret").reshape(Bp, Tp, -1)
    pq = _proj(hp, w_in_l, (o_dq, o_dk), out_dtype=BF16, n_q_cols=diff_w, q_mult=q_mult,
               name="proj_q").reshape(Bp, Tp, diff_w)
    pk = _proj(hp, w_in_l, (o_dk, o_dv), out_dtype=F32, name="proj_k")
    pv = _proj(hp, w_in_l, (o_dv, o_gate), out_dtype=F32, name="proj_v")
    pg = _proj(hp, w_in_l, (o_gate, in_w), out_dtype=BF16, name="proj_gates")
    ret_g, new_state_fwd, new_state_bwd = _retention(pa, lg, None, emit_state=True, dk=dk, dv=dv, **ret_offs)
    diff_n = _diffattn((pq, 0), (pk.reshape(Bp, Tp, diff_w), 0), (pv.reshape(Bp, Tp, diff_w), 0),
                       lam_params, g_subln_l, None, lam_init=lam_init, dh=dh)
    y_prompt = tail(xp2, Bp, Tp, None, ret_g, diff_n, pg, 0)
    new_cache_k = pk.reshape(Bp, 1, Tp, N_DIFF_HEADS, 2, dh)
    new_cache_v = pv.reshape(Bp, 1, Tp, N_DIFF_HEADS, 2 * dh)

    xs2 = x_sample.reshape(Bs * Ts, D)
    hs = _prenorm(xs2, mod, g_mix_pre_l, rows_per_group=Ts)
    sa = _proj(hs, w_in_l, (0, o_dq), out_dtype=BF16, tm=2048, name="proj_ret").reshape(Bs, Ts, -1)
    sqk = _proj(hs, w_in_l, (o_dq, o_dv), out_dtype=BF16, n_q_cols=diff_w, q_mult=q_mult,
                rope_tables=_rope_tables(Ts, dh // 2), name="proj_qk_rope").reshape(Bs, Ts, 2 * diff_w)
    svg = _proj(hs, w_in_l, (o_dv, in_w), out_dtype=BF16, tm=2048, name="proj_vg")
    (ret_g,) = _retention(sa, lg, (state_ret_fwd, state_ret_bwd), emit_state=False, dk=dk, dv=dv, **ret_offs)
    caches = (cache_k[:, l].reshape(Bs, -1, diff_w), cache_v[:, l].reshape(Bs, -1, diff_w))
    diff_n = _diffattn((sqk, 0), (sqk, diff_w), (svg.reshape(Bs, Ts, -1), 0),
                       lam_params, g_subln_l, caches, lam_init=lam_init, dh=dh)
    y_sample = tail(xs2, Bs, Ts, Ts, ret_g, diff_n, svg, diff_w)

    return (y_prompt, y_sample, new_cache_k, new_cache_v, new_state_fwd, new_state_bwd)
```

```python
import functools
import math

import jax
import jax.numpy as jnp
from jax import lax
from jax.experimental import pallas as pl
from jax.experimental.pallas import tpu as pltpu

F32 = jnp.float32
BF16 = jnp.bfloat16

N_RET_HEADS = 8
N_DIFF_HEADS = 8
N_MOD = 6
GRID_W = 64
ROPE_BASE = 10000.0
EPS = 1e-6
SUBLN_EPS = 1e-5

V7X_LANES = 128
V7X_VMEM_BYTES = 64 * 1024 * 1024
VMEM_RESERVE_BYTES = 8 * 1024 * 1024


def _vmem_limit(block_bytes, scratch_bytes=0, temp_bytes=0):
    want = 2 * block_bytes + scratch_bytes + temp_bytes + VMEM_RESERVE_BYTES
    return int(min(want, V7X_VMEM_BYTES - VMEM_RESERVE_BYTES // 2))


def _nbytes(shape, dtype):
    return math.prod(shape) * jnp.dtype(dtype).itemsize


def _mod_map(n_groups, rows_per_group, tm):
    if rows_per_group is None:
        return lambda i, *_: (n_groups - 1, 0, 0)
    return lambda i, *_: ((i * tm) // rows_per_group, 0, 0)


def _adaln_kernel(c_ref, w_ref, b_ref, o_ref):
    c = c_ref[...]
    s = (c * jax.nn.sigmoid(c)).astype(BF16)
    o_ref[...] = jnp.dot(s, w_ref[...].astype(BF16), preferred_element_type=F32) + b_ref[...]


def _adaln(cond, w_ada, b_ada, tn=512):
    R, D = cond.shape
    N = w_ada.shape[1]
    blocks = _nbytes((R, D), F32) + _nbytes((D, tn), F32) + _nbytes((1, tn), F32) + _nbytes((R, tn), F32)
    return pl.pallas_call(
        _adaln_kernel,
        out_shape=jax.ShapeDtypeStruct((R, N), F32),
        grid=(N // tn,),
        in_specs=[pl.BlockSpec((R, D), lambda j: (0, 0)),
                  pl.BlockSpec((D, tn), lambda j: (0, j)),
                  pl.BlockSpec((1, tn), lambda j: (0, j))],
        out_specs=pl.BlockSpec((R, tn), lambda j: (0, j)),
        compiler_params=pltpu.CompilerParams(
            dimension_semantics=("arbitrary",),
            vmem_limit_bytes=_vmem_limit(blocks, temp_bytes=_nbytes((D, tn), BF16))),
        name="adaln",
    )(cond, w_ada, b_ada)


ROW_CHUNK = 64
NORM_UNROLL = 4


def _norm_modulate(x_ref, gain, shift, scale, h_ref):
    a = gain * (1.0 + scale)
    n = x_ref.shape[0] // ROW_CHUNK

    def body(r, carry):
        sl = pl.ds(pl.multiple_of(r * ROW_CHUNK, ROW_CHUNK), ROW_CHUNK)
        x = x_ref[sl, :]
        ms = jnp.mean(x * x, axis=-1, keepdims=True)
        h_ref[sl, :] = (x * lax.rsqrt(ms + EPS) * a + shift).astype(h_ref.dtype)
        return carry

    lax.fori_loop(0, n, body, 0, unroll=NORM_UNROLL)


def _prenorm_kernel(x_ref, mod_ref, g_ref, h_ref):
    _norm_modulate(x_ref, g_ref[...], mod_ref[0:1, :], mod_ref[1:2, :], h_ref)


def _prenorm(x2, mod, gain, *, rows_per_group, tm=1024):
    M, D = x2.shape
    blocks = _nbytes((tm, D), F32) + _nbytes((tm, D), BF16) + _nbytes((N_MOD + 1, D), F32)
    return pl.pallas_call(
        _prenorm_kernel,
        out_shape=jax.ShapeDtypeStruct((M, D), BF16),
        grid=(M // tm,),
        in_specs=[pl.BlockSpec((tm, D), lambda i: (i, 0)),
                  pl.BlockSpec((None, N_MOD, D), _mod_map(mod.shape[0], rows_per_group, tm)),
                  pl.BlockSpec((1, D), lambda i: (0, 0))],
        out_specs=pl.BlockSpec((tm, D), lambda i: (i, 0)),
        compiler_params=pltpu.CompilerParams(
            dimension_semantics=("parallel",),
            vmem_limit_bytes=_vmem_limit(blocks, temp_bytes=4 * _nbytes((ROW_CHUNK, D), F32))),
        name="prenorm",
    )(x2, mod, gain)


CAST_ROWS = 256


def _proj_kernel(h_ref, w_ref, *rest, n_q_blocks, q_mult, rope):
    if rope:
        cos_ref, sa_ref, sb_ref, o_ref, wb_ref = rest
    else:
        o_ref, wb_ref = rest

    @pl.when(pl.program_id(1) == 0)
    def _():
        def body(r, carry):
            sl = pl.ds(pl.multiple_of(r * CAST_ROWS, CAST_ROWS), CAST_ROWS)
            wb_ref[sl, :] = w_ref[sl, :].astype(wb_ref.dtype)
            return carry

        lax.fori_loop(0, w_ref.shape[0] // CAST_ROWS, body, 0)

    acc = jnp.dot(h_ref[...], wb_ref[...], preferred_element_type=F32)
    mult = None
    if n_q_blocks:
        mult = jnp.where(pl.program_id(0) < n_q_blocks, q_mult, 1.0).astype(F32)
    if rope:
        cos, sa, sb = cos_ref[...] * mult, sa_ref[...] * mult, sb_ref[...] * mult
        for c in range(acc.shape[1] // V7X_LANES):
            cs = slice(c * V7X_LANES, (c + 1) * V7X_LANES)
            xc = acc[:, cs]
            y = xc * cos + pltpu.roll(xc, 96, 1) * sa + pltpu.roll(xc, 32, 1) * sb
            o_ref[:, cs] = y.astype(o_ref.dtype)
    elif mult is not None:
        o_ref[...] = (acc * mult).astype(o_ref.dtype)
    else:
        o_ref[...] = acc.astype(o_ref.dtype)


def _proj(h, w, cols, *, out_dtype, n_q_cols=0, q_mult=1.0, rope_tables=None, tm=1024, tn=1024, name="proj"):
    M, D = h.shape
    N = cols[1] - cols[0]
    tm = min(tm, M)
    tn = min(tn, N)
    j0 = cols[0] // tn
    in_specs = [pl.BlockSpec((tm, D), lambda j, i: (i, 0)),
                pl.BlockSpec((D, tn), lambda j, i: (0, j0 + j))]
    args = [h, w]
    blocks = _nbytes((tm, D), BF16) + _nbytes((D, tn), w.dtype) + _nbytes((tm, tn), out_dtype)
    if rope_tables is not None:
        nt = rope_tables[0].shape[0] // tm
        for t in rope_tables:
            in_specs.append(pl.BlockSpec((tm, V7X_LANES), lambda j, i: (i % nt, 0)))
            args.append(t)
        blocks += 3 * _nbytes((tm, V7X_LANES), F32)
    return pl.pallas_call(
        functools.partial(_proj_kernel, n_q_blocks=n_q_cols // tn, q_mult=q_mult, rope=rope_tables is not None),
        out_shape=jax.ShapeDtypeStruct((M, N), out_dtype),
        grid=(N // tn, M // tm),
        in_specs=in_specs,
        out_specs=pl.BlockSpec((tm, tn), lambda j, i: (i, j)),
        scratch_shapes=[pltpu.VMEM((D, tn), BF16)],
        compiler_params=pltpu.CompilerParams(
            dimension_semantics=("parallel", "arbitrary"),
            vmem_limit_bytes=_vmem_limit(blocks, _nbytes((D, tn), BF16), _nbytes((tm, tn), F32))),
        name=name,
    )(*args)


RET_CHUNK = 256


def _log_sigmoid(x):
    return jnp.minimum(x, 0.0) - jnp.log1p(jnp.exp(-jnp.abs(x)))


def _ret_kernel(lg_ref, q_ref, k_ref, v_ref, g_ref, *rest, has_state, emit_state, k_scale):
    rest = list(rest)
    s0f_ref = s0b_ref = sf_out = sb_out = None
    if has_state:
        s0f_ref, s0b_ref = rest[:2]
        rest = rest[2:]
    o_ref = rest[0]
    rest = rest[1:]
    if emit_state:
        sf_out, sb_out = rest[:2]
        rest = rest[2:]
    (st_scr,) = rest

    bb, T, dk = q_ref.shape
    C = min(RET_CHUNK, T)
    n = T // C

    lsig = _log_sigmoid(lg_ref[...])
    lgf = lsig[0:1, 0:1]
    lgb = lsig[1:2, 0:1]
    ri = lax.broadcasted_iota(jnp.int32, (C, C), 0)
    ci = lax.broadcasted_iota(jnp.int32, (C, C), 1)
    rel = (ri - ci).astype(F32)
    decay = jnp.where(rel > 0, jnp.exp(rel * lgf), jnp.where(rel < 0, jnp.exp(-rel * lgb), 2.0)) * k_scale
    li = lax.broadcasted_iota(jnp.int32, (C, 1), 0).astype(F32)
    xi_f = jnp.exp((li + 1.0) * lgf)
    xi_b = jnp.exp((C - li) * lgb)
    zeta_f = jnp.exp((C - 1.0 - li) * lgf) * k_scale
    zeta_b = jnp.exp(li * lgb) * k_scale
    gc_f = jnp.exp(C * lgf)
    gc_b = jnp.exp(C * lgb)

    def rows(c):
        return slice(c * C, (c + 1) * C)

    def fwd_state_used(c):
        return has_state or c > 0

    def bwd_state_used(c):
        return has_state or c < n - 1

    def kv_pair(item):
        b, t = item
        out = []
        for c, zeta in ((t, zeta_f), (n - 1 - t, zeta_b)):
            ks = (k_ref[b, rows(c), :].astype(F32) * zeta).astype(BF16)
            out.append(lax.dot_general(ks, v_ref[b, rows(c), :], (((0,), (0,)), ((), ())),
                                       preferred_element_type=F32))
        return out

    n_upd = n if emit_state else n - 1
    items = [(b, t) for b in range(bb) for t in range(n_upd)]
    kv_next = kv_pair(items[0]) if items else None
    pos = 0
    for b in range(bb):
        sf = s0f_ref[b] if has_state else None
        sb = s0b_ref[b] if has_state else None
        for t in range(n):
            cf, cb = t, n - 1 - t
            if fwd_state_used(cf):
                st_scr[b, cf, 0:dk, :] = sf.astype(BF16)
            if bwd_state_used(cb):
                st_scr[b, cb, dk:2 * dk, :] = sb.astype(BF16)
            if t < n_upd:
                kvf, kvb = kv_next
                pos += 1
                if pos < len(items):
                    kv_next = kv_pair(items[pos])
                sf = kvf if sf is None else gc_f * sf + kvf
                sb = kvb if sb is None else gc_b * sb + kvb
        if emit_state:
            sf_out[b] = sf
            sb_out[b] = sb

    def chunk_out(item):
        b, c = item
        qc = q_ref[b, rows(c), :]
        s = lax.dot_general(qc, k_ref[b, rows(c), :], (((1,), (1,)), ((), ())),
                            preferred_element_type=F32) * decay
        o = jnp.dot(s.astype(BF16), v_ref[b, rows(c), :], preferred_element_type=F32)
        qf = qc.astype(F32)
        parts = []
        if fwd_state_used(c):
            parts.append(((qf * xi_f).astype(BF16), st_scr[b, c, 0:dk, :]))
        if bwd_state_used(c):
            parts.append(((qf * xi_b).astype(BF16), st_scr[b, c, dk:2 * dk, :]))
        if len(parts) == 2:
            o = o + jnp.dot(jnp.concatenate([parts[0][0], parts[1][0]], axis=1), st_scr[b, c],
                            preferred_element_type=F32)
        elif parts:
            o = o + jnp.dot(parts[0][0], parts[0][1], preferred_element_type=F32)
        return o

    items = [(b, c) for b in range(bb) for c in range(n)]
    o_next = chunk_out(items[0])
    for pos, (b, c) in enumerate(items):
        o = o_next
        if pos + 1 < len(items):
            o_next = chunk_out(items[pos + 1])
        on = o * lax.rsqrt(jnp.mean(o * o, axis=-1, keepdims=True) + EPS)
        g = g_ref[b, rows(c), :].astype(F32)
        o_ref[b, rows(c), :] = (g * jax.nn.sigmoid(g) * on).astype(o_ref.dtype)


def _retention(pa3, lg, states, *, emit_state, dk, dv, off_q, off_k, off_v, off_g, batch_block=1):
    B, T, _ = pa3.shape
    H = N_RET_HEADS
    bb = batch_block
    has_state = states is not None
    C = min(RET_CHUNK, T)
    n = T // C
    in_specs = [pl.BlockSpec((None, 8, V7X_LANES), lambda b, h: (h, 0, 0)),
                pl.BlockSpec((bb, T, dk), lambda b, h: (b, 0, off_q // dk + h)),
                pl.BlockSpec((bb, T, dk), lambda b, h: (b, 0, off_k // dk + h)),
                pl.BlockSpec((bb, T, dv), lambda b, h: (b, 0, off_v // dv + h)),
                pl.BlockSpec((bb, T, dv), lambda b, h: (b, 0, off_g // dv + h))]
    args = [lg, pa3, pa3, pa3, pa3]
    blocks = bb * (2 * _nbytes((T, dk), BF16) + 3 * _nbytes((T, dv), BF16)) + _nbytes((8, V7X_LANES), F32)
    state_spec = pl.BlockSpec((bb, None, None, dk, dv), lambda b, h: (b, 0, h, 0, 0))
    if has_state:
        in_specs += [state_spec, state_spec]
        args += list(states)
        blocks += 2 * bb * _nbytes((dk, dv), F32)
    out_shape = [jax.ShapeDtypeStruct((B, T, H * dv), BF16)]
    out_specs = [pl.BlockSpec((bb, T, dv), lambda b, h: (b, 0, h))]
    if emit_state:
        out_shape += [jax.ShapeDtypeStruct((B, 1, H, dk, dv), F32)] * 2
        out_specs += [state_spec, state_spec]
        blocks += 2 * bb * _nbytes((dk, dv), F32)
    scratch_bytes = _nbytes((bb, n, 2 * dk, dv), BF16)
    temps = 8 * _nbytes((C, dv), F32) + 8 * _nbytes((dk, dv), F32) + 4 * _nbytes((C, C), F32)
    return pl.pallas_call(
        functools.partial(_ret_kernel, has_state=has_state, emit_state=emit_state, k_scale=float(dk) ** -0.5),
        out_shape=out_shape,
        grid=(B // bb, H),
        in_specs=in_specs,
        out_specs=out_specs,
        scratch_shapes=[pltpu.VMEM((bb, n, 2 * dk, dv), BF16)],
        compiler_params=pltpu.CompilerParams(
            dimension_semantics=("parallel", "parallel"),
            vmem_limit_bytes=_vmem_limit(blocks, scratch_bytes, temps)),
        name="retention",
    )(*args)


DIFF_SUB_ROWS = 256


def _diff_kernel(lam_ref, gs_ref, q_ref, k_ref, v_ref, *rest, has_cache, lam_init, dh, sub):
    if has_cache:
        ck_ref, cv_ref, o_ref, kall, vall = rest
    else:
        o_ref, kall, vall = rest
    T = k_ref.shape[0]
    w = 2 * dh
    heads = q_ref.shape[1] // w

    kall[0:T, :] = k_ref[...].astype(kall.dtype)
    vall[0:T, :] = v_ref[...].astype(vall.dtype)
    if has_cache:
        kall[T:, :] = ck_ref[...].astype(kall.dtype)
        vall[T:, :] = cv_ref[...].astype(vall.dtype)

    lp = lam_ref[...]
    lam = (jnp.exp(jnp.sum(lp[0:1, :] * lp[1:2, :], axis=-1, keepdims=True))
           - jnp.exp(jnp.sum(lp[2:3, :] * lp[3:4, :], axis=-1, keepdims=True)) + lam_init)
    gain = gs_ref[...] * (1.0 - lam_init)
    items = [(hh, r) for hh in range(heads) for r in range(T // sub)]

    def scores(item):
        hh, r = item
        out = []
        for m in range(2):
            cs = slice(hh * w + m * dh, hh * w + (m + 1) * dh)
            out.append(lax.dot_general(q_ref[r * sub:(r + 1) * sub, cs], kall[:, cs],
                                       (((1,), (1,)), ((), ())), preferred_element_type=F32))
        return out

    s_next = scores(items[0])
    for t, (hh, r) in enumerate(items):
        s_cur = s_next
        if t + 1 < len(items):
            s_next = scores(items[t + 1])
        probs = []
        for s in s_cur:
            e = jnp.exp2(s - jnp.max(s, axis=-1, keepdims=True))
            probs.append((e, jnp.sum(e, axis=-1, keepdims=True)))
        (e1, l1), (e2, l2) = probs
        r1 = 1.0 / l1
        a = (e1 - e2 * (lam * l1 / l2)).astype(BF16)
        o = jnp.dot(a, vall[:, hh * w:(hh + 1) * w], preferred_element_type=F32) * r1
        on = o * lax.rsqrt(jnp.mean(o * o, axis=-1, keepdims=True) + SUBLN_EPS)
        o_ref[r * sub:(r + 1) * sub, hh * w:(hh + 1) * w] = (on * gain).astype(o_ref.dtype)


def _diffattn(q_src, k_src, v_src, lam_params, g_subln, caches, *, lam_init, dh, heads_per_step):
    (qa, off_q), (ka, off_k), (va, off_v) = q_src, k_src, v_src
    B, T, _ = qa.shape
    H = N_DIFF_HEADS
    w = 2 * dh
    gw = heads_per_step * w
    sub = min(DIFF_SUB_ROWS, T)
    has_cache = caches is not None
    Tk = T + (caches[0].shape[1] if has_cache else 0)
    in_specs = [pl.BlockSpec((8, dh), lambda b, g: (0, 0)),
                pl.BlockSpec((1, w), lambda b, g: (0, 0)),
                pl.BlockSpec((None, T, gw), lambda b, g: (b, 0, off_q // gw + g)),
                pl.BlockSpec((None, T, gw), lambda b, g: (b, 0, off_k // gw + g)),
                pl.BlockSpec((None, T, gw), lambda b, g: (b, 0, off_v // gw + g))]
    args = [lam_params, g_subln, qa, ka, va]
    blocks = 2 * _nbytes((T, gw), BF16) + _nbytes((T, gw), ka.dtype) + _nbytes((T, gw), va.dtype)
    if has_cache:
        P = caches[0].shape[1]
        cspec = pl.BlockSpec((None, P, gw), lambda b, g: (b, 0, g))
        in_specs += [cspec, cspec]
        args += list(caches)
        blocks += 2 * _nbytes((P, gw), F32)
    temps = 10 * _nbytes((sub, Tk), F32)
    return pl.pallas_call(
        functools.partial(_diff_kernel, has_cache=has_cache, lam_init=lam_init, dh=dh, sub=sub),
        out_shape=jax.ShapeDtypeStruct((B, T, H * w), BF16),
        grid=(B, H // heads_per_step),
        in_specs=in_specs,
        out_specs=pl.BlockSpec((None, T, gw), lambda b, g: (b, 0, g)),
        scratch_shapes=[pltpu.VMEM((Tk, gw), BF16), pltpu.VMEM((Tk, gw), BF16)],
        compiler_params=pltpu.CompilerParams(
            dimension_semantics=("parallel", "parallel"),
            vmem_limit_bytes=_vmem_limit(blocks, 2 * _nbytes((Tk, gw), BF16), temps)),
        name="diffattn",
    )(*args)


def _merge_a_kernel(r_ref, d_ref, wr_ref, wd_ref, gr_ref, gd_ref, o_ref):
    rb = jnp.dot(r_ref[...], wr_ref[...], preferred_element_type=F32)
    db = jnp.dot(d_ref[...], wd_ref[...], preferred_element_type=F32)
    gr = jax.nn.sigmoid(gr_ref[...].astype(F32))
    gd = jax.nn.sigmoid(gd_ref[...].astype(F32))
    o_ref[...] = (gr * rb + gd * db).astype(o_ref.dtype)


def _merge_a(ret_g, diff_n, w_ret_o, w_diff_o, gates, *, off_gr, off_gd, tm=1024, tn=512):
    M, Kr = ret_g.shape
    Kd = diff_n.shape[1]
    D = w_ret_o.shape[1]
    tm = min(tm, M)
    blocks = (_nbytes((tm, Kr), BF16) + _nbytes((tm, Kd), BF16) + _nbytes((Kr, tn), BF16)
              + _nbytes((Kd, tn), BF16) + 3 * _nbytes((tm, tn), BF16))
    return pl.pallas_call(
        _merge_a_kernel,
        out_shape=jax.ShapeDtypeStruct((M, D), BF16),
        grid=(M // tm, D // tn),
        in_specs=[pl.BlockSpec((tm, Kr), lambda i, j: (i, 0)),
                  pl.BlockSpec((tm, Kd), lambda i, j: (i, 0)),
                  pl.BlockSpec((Kr, tn), lambda i, j: (0, j)),
                  pl.BlockSpec((Kd, tn), lambda i, j: (0, j)),
                  pl.BlockSpec((tm, tn), lambda i, j: (i, off_gr // tn + j)),
                  pl.BlockSpec((tm, tn), lambda i, j: (i, off_gd // tn + j))],
        out_specs=pl.BlockSpec((tm, tn), lambda i, j: (i, j)),
        compiler_params=pltpu.CompilerParams(
            dimension_semantics=("parallel", "arbitrary"),
            vmem_limit_bytes=_vmem_limit(blocks, temp_bytes=4 * _nbytes((tm, tn), F32))),
        name="merge_a",
    )(ret_g, diff_n, w_ret_o, w_diff_o, gates, gates)


MERGE_SUB_ROWS = 256


def _merge_b_kernel(m_ref, w_ref, x_ref, mod_ref, gpost_ref, gpre_ref, x1_ref, h2_ref):
    gate_post = mod_ref[2:3, :] * gpost_ref[...]
    a2 = gpre_ref[...] * (1.0 + mod_ref[4:5, :])
    shift2 = mod_ref[3:4, :]
    sub = MERGE_SUB_ROWS
    n_sub = x_ref.shape[0] // sub

    def out_proj(k):
        return jnp.dot(m_ref[k * sub:(k + 1) * sub, :], w_ref[...], preferred_element_type=F32)

    y_next = out_proj(0)
    for k in range(n_sub):
        y_all = y_next
        if k + 1 < n_sub:
            y_next = out_proj(k + 1)
        for c in range(sub // ROW_CHUNK):
            sl = slice(k * sub + c * ROW_CHUNK, k * sub + (c + 1) * ROW_CHUNK)
            y = y_all[c * ROW_CHUNK:(c + 1) * ROW_CHUNK, :]
            x1 = x_ref[sl, :] + y * lax.rsqrt(jnp.mean(y * y, axis=-1, keepdims=True) + EPS) * gate_post
            x1_ref[sl, :] = x1
            h2_ref[sl, :] = (x1 * lax.rsqrt(jnp.mean(x1 * x1, axis=-1, keepdims=True) + EPS) * a2
                             + shift2).astype(h2_ref.dtype)


def _merge_b(merged, w_out, x2, mod, g_post, g_pre, *, rows_per_group, tm=512):
    M, D = x2.shape
    tm = min(tm, M)
    blocks = (_nbytes((tm, D), BF16) * 2 + _nbytes((D, D), BF16) + 2 * _nbytes((tm, D), F32)
              + _nbytes((N_MOD + 2, D), F32))
    return pl.pallas_call(
        _merge_b_kernel,
        out_shape=[jax.ShapeDtypeStruct((M, D), F32), jax.ShapeDtypeStruct((M, D), BF16)],
        grid=(M // tm,),
        in_specs=[pl.BlockSpec((tm, D), lambda i: (i, 0)),
                  pl.BlockSpec((D, D), lambda i: (0, 0)),
                  pl.BlockSpec((tm, D), lambda i: (i, 0)),
                  pl.BlockSpec((None, N_MOD, D), _mod_map(mod.shape[0], rows_per_group, tm)),
                  pl.BlockSpec((1, D), lambda i: (0, 0)),
                  pl.BlockSpec((1, D), lambda i: (0, 0))],
        out_specs=[pl.BlockSpec((tm, D), lambda i: (i, 0)),
                   pl.BlockSpec((tm, D), lambda i: (i, 0))],
        compiler_params=pltpu.CompilerParams(
            dimension_semantics=("parallel",),
            vmem_limit_bytes=_vmem_limit(blocks, temp_bytes=2 * _nbytes((tm, D), F32))),
        name="merge_b",
    )(merged, w_out, x2, mod, g_post, g_pre)


def _mlp_kernel(h_ref, wu_ref, wd_ref, x_ref, mod_ref, g_ref, o_ref, acc_ref):
    j = pl.program_id(1)

    @pl.when(j == 0)
    def _():
        acc_ref[...] = jnp.zeros_like(acc_ref)

    u = jnp.maximum(jnp.dot(h_ref[...], wu_ref[...], preferred_element_type=F32), 0.0)
    acc_ref[...] += jnp.dot((u * u).astype(BF16), wd_ref[...], preferred_element_type=F32)

    @pl.when(j == pl.num_programs(1) - 1)
    def _():
        gate_post = mod_ref[5:6, :] * g_ref[...]
        n = x_ref.shape[0] // ROW_CHUNK

        def body(r, carry):
            sl = pl.ds(pl.multiple_of(r * ROW_CHUNK, ROW_CHUNK), ROW_CHUNK)
            y = acc_ref[sl, :]
            o_ref[sl, :] = x_ref[sl, :] + y * lax.rsqrt(jnp.mean(y * y, axis=-1, keepdims=True) + EPS) * gate_post
            return carry

        lax.fori_loop(0, n, body, 0, unroll=NORM_UNROLL)


def _mlp(h2, w_up, w_down, x1, mod, g_post, *, rows_per_group, tm=512, tf=1024):
    M, D = x1.shape
    Fh = w_up.shape[1]
    tm = min(tm, M)
    blocks = (_nbytes((tm, D), BF16) + 2 * _nbytes((D, tf), BF16) + 2 * _nbytes((tm, D), F32)
              + _nbytes((N_MOD + 1, D), F32))
    return pl.pallas_call(
        _mlp_kernel,
        out_shape=jax.ShapeDtypeStruct((M, D), F32),
        grid=(M // tm, Fh // tf),
        in_specs=[pl.BlockSpec((tm, D), lambda i, j: (i, 0)),
                  pl.BlockSpec((D, tf), lambda i, j: (0, j)),
                  pl.BlockSpec((tf, D), lambda i, j: (j, 0)),
                  pl.BlockSpec((tm, D), lambda i, j: (i, 0)),
                  pl.BlockSpec((None, N_MOD, D), _mod_map(mod.shape[0], rows_per_group, tm)),
                  pl.BlockSpec((1, D), lambda i, j: (0, 0))],
        out_specs=pl.BlockSpec((tm, D), lambda i, j: (i, 0)),
        scratch_shapes=[pltpu.VMEM((tm, D), F32)],
        compiler_params=pltpu.CompilerParams(
            dimension_semantics=("parallel", "arbitrary"),
            vmem_limit_bytes=_vmem_limit(blocks, _nbytes((tm, D), F32),
                                         2 * _nbytes((tm, tf), F32) + _nbytes((tm, D), F32))),
        name="mlp",
    )(h2, w_up, w_down, x1, mod, g_post)


def _rope_tables(T, rope_half):
    rows = T // GRID_W
    row = jnp.repeat(jnp.arange(rows, dtype=F32), GRID_W)
    col = jnp.tile(jnp.arange(GRID_W, dtype=F32), rows)
    inv = ROPE_BASE ** (-jnp.arange(0, rope_half, 2, dtype=F32) / rope_half)
    ar, ac = row[:, None] * inv, col[:, None] * inv
    z = jnp.zeros_like(ar)
    cos = jnp.concatenate([jnp.cos(ar), jnp.cos(ar), jnp.cos(ac), jnp.cos(ac)], axis=-1)
    sin_next = jnp.concatenate([-jnp.sin(ar), z, -jnp.sin(ac), z], axis=-1)
    sin_prev = jnp.concatenate([z, jnp.sin(ar), z, jnp.sin(ac)], axis=-1)
    return cos, sin_next, sin_prev


def kernel(x_prompt, x_sample, cache_k, cache_v, state_ret_fwd, state_ret_bwd, c, c_ctx, w_ada, b_ada, g_mix_pre, g_mix_post, g_mlp_pre, g_mlp_post, w_in, ret_gamma_logit_fwd, ret_gamma_logit_bwd, w_ret_o, lambda_q1, lambda_k1, lambda_q2, lambda_k2, g_diff_subln, w_diff_o, w_out, w_mlp_up, w_mlp_down):
    Bp, Tp, D = x_prompt.shape
    Bs, Ts, _ = x_sample.shape
    depth = w_in.shape[0]
    assert depth == 1
    l = 0
    H = N_RET_HEADS
    dk = D // H
    dv = 2 * dk
    dh = D // N_DIFF_HEADS // 2
    ret_qk_w, ret_v_w, diff_w = H * dk, H * dv, N_DIFF_HEADS * 2 * dh
    o_dq = 2 * ret_qk_w + 2 * ret_v_w
    o_dk, o_dv, o_gate = o_dq + diff_w, o_dq + 2 * diff_w, o_dq + 3 * diff_w
    in_w = o_gate + 2 * D
    assert w_in.shape[2] == in_w
    lam_init = 0.8 - 0.6 * math.exp(-0.3 * l)
    q_mult = float(dh) ** -0.5 * math.log2(math.e)

    w_in_l = w_in[l]
    w_ret_o_b = w_ret_o[l].astype(BF16)
    w_diff_o_b = w_diff_o[l].astype(BF16)
    w_out_b = w_out[l].astype(BF16)
    w_up_b = w_mlp_up[l].astype(BF16)
    w_down_b = w_mlp_down[l].astype(BF16)
    g_mix_pre_l = g_mix_pre[l].reshape(1, D)
    g_mix_post_l = g_mix_post[l].reshape(1, D)
    g_mlp_pre_l = g_mlp_pre[l].reshape(1, D)
    g_mlp_post_l = g_mlp_post[l].reshape(1, D)
    g_subln_l = g_diff_subln[l].reshape(1, 2 * dh)
    lam_params = jnp.concatenate(
        [jnp.stack([lambda_q1[l], lambda_k1[l], lambda_q2[l], lambda_k2[l]]), jnp.zeros((4, dh), F32)], axis=0)
    lg = jnp.stack([ret_gamma_logit_fwd[l], ret_gamma_logit_bwd[l]], axis=1)
    lg = jnp.concatenate([lg, jnp.zeros((H, 6), F32)], axis=1)
    lg = jnp.broadcast_to(lg[:, :, None], (H, 8, V7X_LANES))

    n_cond = 16
    cond = jnp.concatenate([c, jnp.broadcast_to(c_ctx[None, :], (n_cond - Bs, D))], axis=0)
    mod = _adaln(cond, w_ada[l], b_ada[l].reshape(1, N_MOD * D)).reshape(n_cond, N_MOD, D)

    ret_offs = dict(off_q=0, off_k=ret_qk_w, off_v=2 * ret_qk_w, off_g=2 * ret_qk_w + ret_v_w)

    def tail(x2, B, T, rows_per_group, ret_g, diff_n, gates, off_gr):
        merged = _merge_a(ret_g.reshape(B * T, ret_v_w), diff_n.reshape(B * T, diff_w), w_ret_o_b, w_diff_o_b,
                          gates, off_gr=off_gr, off_gd=off_gr + D)
        x1, h2 = _merge_b(merged, w_out_b, x2, mod, g_mix_post_l, g_mlp_pre_l, rows_per_group=rows_per_group)
        y = _mlp(h2, w_up_b, w_down_b, x1, mod, g_mlp_post_l, rows_per_group=rows_per_group)
        return y.reshape(B, T, D)

    xp2 = x_prompt.reshape(Bp * Tp, D)
    hp = _prenorm(xp2, mod, g_mix_pre_l, rows_per_group=None)
    pa = _proj(hp, w_in_l, (0, o_dq), out_dtype=BF16, name="proj_ret").reshape(Bp, Tp, -1)
    pq = _proj(hp, w_in_l, (o_dq, o_dk), out_dtype=BF16, n_q_cols=diff_w, q_mult=q_mult,
               name="proj_q").reshape(Bp, Tp, diff_w)
    pk = _proj(hp, w_in_l, (o_dk, o_dv), out_dtype=F32, name="proj_k")
    pv = _proj(hp, w_in_l, (o_dv, o_gate), out_dtype=F32, name="proj_v")
    pg = _proj(hp, w_in_l, (o_gate, in_w), out_dtype=BF16, name="proj_gates")
    ret_g, new_state_fwd, new_state_bwd = _retention(pa, lg, None, emit_state=True, dk=dk, dv=dv,
                                                     batch_block=4, **ret_offs)
    diff_n = _diffattn((pq, 0), (pk.reshape(Bp, Tp, diff_w), 0), (pv.reshape(Bp, Tp, diff_w), 0),
                       lam_params, g_subln_l, None, lam_init=lam_init, dh=dh, heads_per_step=N_DIFF_HEADS)
    y_prompt = tail(xp2, Bp, Tp, None, ret_g, diff_n, pg, 0)
    new_cache_k = pk.reshape(Bp, 1, Tp, N_DIFF_HEADS, 2, dh)
    new_cache_v = pv.reshape(Bp, 1, Tp, N_DIFF_HEADS, 2 * dh)

    xs2 = x_sample.reshape(Bs * Ts, D)
    hs = _prenorm(xs2, mod, g_mix_pre_l, rows_per_group=Ts)
    sa = _proj(hs, w_in_l, (0, o_dq), out_dtype=BF16, tm=2048, name="proj_ret").reshape(Bs, Ts, -1)
    sqk = _proj(hs, w_in_l, (o_dq, o_dv), out_dtype=BF16, n_q_cols=diff_w, q_mult=q_mult,
                rope_tables=_rope_tables(Ts, dh // 2), name="proj_qk_rope").reshape(Bs, Ts, 2 * diff_w)
    svg = _proj(hs, w_in_l, (o_dv, in_w), out_dtype=BF16, tm=2048, name="proj_vg")
    (ret_g,) = _retention(sa, lg, (state_ret_fwd, state_ret_bwd), emit_state=False, dk=dk, dv=dv, **ret_offs)
    caches = (cache_k[:, l].reshape(Bs, -1, diff_w), cache_v[:, l].reshape(Bs, -1, diff_w))
    diff_n = _diffattn((sqk, 0), (sqk, diff_w), (svg.reshape(Bs, Ts, -1), 0),
                       lam_params, g_subln_l, caches, lam_init=lam_init, dh=dh, heads_per_step=1)
    y_sample = tail(xs2, Bs, Ts, Ts, ret_g, diff_n, svg, diff_w)

    return (y_prompt, y_sample, new_cache_k, new_cache_v, new_state_fwd, new_state_bwd)
```

```python
import functools
import math

import jax
import jax.numpy as jnp
from jax import lax
from jax.experimental import pallas as pl
from jax.experimental.pallas import tpu as pltpu

F32 = jnp.float32
BF16 = jnp.bfloat16

N_RET_HEADS = 8
N_DIFF_HEADS = 8
N_MOD = 6
GRID_W = 64
ROPE_BASE = 10000.0
EPS = 1e-6
SUBLN_EPS = 1e-5

V7X_LANES = 128
V7X_VMEM_BYTES = 64 * 1024 * 1024
VMEM_RESERVE_BYTES = 8 * 1024 * 1024


def _vmem_limit(block_bytes, scratch_bytes=0, temp_bytes=0):
    want = 2 * block_bytes + scratch_bytes + temp_bytes + VMEM_RESERVE_BYTES
    return int(min(want, V7X_VMEM_BYTES - VMEM_RESERVE_BYTES // 2))


def _nbytes(shape, dtype):
    return math.prod(shape) * jnp.dtype(dtype).itemsize


def _mod_map(n_groups, rows_per_group, tm):
    if rows_per_group is None:
        return lambda i, *_: (n_groups - 1, 0, 0)
    return lambda i, *_: ((i * tm) // rows_per_group, 0, 0)


def _adaln_kernel(c_ref, w_ref, b_ref, o_ref):
    c = c_ref[...]
    s = (c * jax.nn.sigmoid(c)).astype(BF16)
    o_ref[...] = jnp.dot(s, w_ref[...].astype(BF16), preferred_element_type=F32) + b_ref[...]


def _adaln(cond, w_ada, b_ada, tn=512):
    R, D = cond.shape
    N = w_ada.shape[1]
    blocks = _nbytes((R, D), F32) + _nbytes((D, tn), F32) + _nbytes((1, tn), F32) + _nbytes((R, tn), F32)
    return pl.pallas_call(
        _adaln_kernel,
        out_shape=jax.ShapeDtypeStruct((R, N), F32),
        grid=(N // tn,),
        in_specs=[pl.BlockSpec((R, D), lambda j: (0, 0)),
                  pl.BlockSpec((D, tn), lambda j: (0, j)),
                  pl.BlockSpec((1, tn), lambda j: (0, j))],
        out_specs=pl.BlockSpec((R, tn), lambda j: (0, j)),
        compiler_params=pltpu.CompilerParams(
            dimension_semantics=("arbitrary",),
            vmem_limit_bytes=_vmem_limit(blocks, temp_bytes=_nbytes((D, tn), BF16))),
        name="adaln",
    )(cond, w_ada, b_ada)


ROW_CHUNK = 64
NORM_UNROLL = 4


def _norm_modulate(x_ref, gain, shift, scale, h_ref):
    a = gain * (1.0 + scale)
    n = x_ref.shape[0] // ROW_CHUNK

    def body(r, carry):
        sl = pl.ds(pl.multiple_of(r * ROW_CHUNK, ROW_CHUNK), ROW_CHUNK)
        x = x_ref[sl, :]
        ms = jnp.mean(x * x, axis=-1, keepdims=True)
        h_ref[sl, :] = (x * lax.rsqrt(ms + EPS) * a + shift).astype(h_ref.dtype)
        return carry

    lax.fori_loop(0, n, body, 0, unroll=NORM_UNROLL)


def _prenorm_kernel(x_ref, mod_ref, g_ref, h_ref):
    _norm_modulate(x_ref, g_ref[...], mod_ref[0:1, :], mod_ref[1:2, :], h_ref)


def _prenorm(x2, mod, gain, *, rows_per_group, tm=1024):
    M, D = x2.shape
    blocks = _nbytes((tm, D), F32) + _nbytes((tm, D), BF16) + _nbytes((N_MOD + 1, D), F32)
    return pl.pallas_call(
        _prenorm_kernel,
        out_shape=jax.ShapeDtypeStruct((M, D), BF16),
        grid=(M // tm,),
        in_specs=[pl.BlockSpec((tm, D), lambda i: (i, 0)),
                  pl.BlockSpec((None, N_MOD, D), _mod_map(mod.shape[0], rows_per_group, tm)),
                  pl.BlockSpec((1, D), lambda i: (0, 0))],
        out_specs=pl.BlockSpec((tm, D), lambda i: (i, 0)),
        compiler_params=pltpu.CompilerParams(
            dimension_semantics=("parallel",),
            vmem_limit_bytes=_vmem_limit(blocks, temp_bytes=4 * _nbytes((ROW_CHUNK, D), F32))),
        name="prenorm",
    )(x2, mod, gain)


CAST_ROWS = 256


def _slab_cast_specs(params, n_steps, step_index):
    in_specs, out_specs, out_shapes, nbytes = [], [], [], 0
    for p in params:
        rows, width = p.shape[0] // n_steps, p.shape[1]
        assert rows * n_steps == p.shape[0] and rows % 16 == 0
        spec = pl.BlockSpec((rows, width), lambda *ids: (step_index(*ids), 0))
        in_specs.append(spec)
        out_specs.append(spec)
        out_shapes.append(jax.ShapeDtypeStruct(p.shape, BF16))
        nbytes += _nbytes((rows, width), F32) + _nbytes((rows, width), BF16)
    return in_specs, out_specs, out_shapes, nbytes


def _proj_kernel(h_ref, w_ref, *rest, n_q_blocks, q_mult, rope, n_cast):
    rest = list(rest)
    if rope:
        cos_ref, sa_ref, sb_ref = rest[:3]
        rest = rest[3:]
    cast_in, o_ref, cast_out, wb_ref = rest[:n_cast], rest[n_cast], rest[n_cast + 1:2 * n_cast + 1], rest[-1]
    for src, dst in zip(cast_in, cast_out):
        dst[...] = src[...].astype(dst.dtype)

    @pl.when(pl.program_id(1) == 0)
    def _():
        def body(r, carry):
            sl = pl.ds(pl.multiple_of(r * CAST_ROWS, CAST_ROWS), CAST_ROWS)
            wb_ref[sl, :] = w_ref[sl, :].astype(wb_ref.dtype)
            return carry

        lax.fori_loop(0, w_ref.shape[0] // CAST_ROWS, body, 0)

    acc = jnp.dot(h_ref[...], wb_ref[...], preferred_element_type=F32)
    mult = None
    if n_q_blocks:
        mult = jnp.where(pl.program_id(0) < n_q_blocks, q_mult, 1.0).astype(F32)
    if rope:
        cos, sa, sb = cos_ref[...] * mult, sa_ref[...] * mult, sb_ref[...] * mult
        for c in range(acc.shape[1] // V7X_LANES):
            cs = slice(c * V7X_LANES, (c + 1) * V7X_LANES)
            xc = acc[:, cs]
            y = xc * cos + pltpu.roll(xc, 96, 1) * sa + pltpu.roll(xc, 32, 1) * sb
            o_ref[:, cs] = y.astype(o_ref.dtype)
    elif mult is not None:
        o_ref[...] = (acc * mult).astype(o_ref.dtype)
    else:
        o_ref[...] = acc.astype(o_ref.dtype)


def _proj(h, w, cols, *, out_dtype, n_q_cols=0, q_mult=1.0, rope_tables=None, cast_params=(),
          tm=1024, tn=1024, name="proj"):
    M, D = h.shape
    N = cols[1] - cols[0]
    tm = min(tm, M)
    tn = min(tn, N)
    j0 = cols[0] // tn
    in_specs = [pl.BlockSpec((tm, D), lambda j, i: (i, 0)),
                pl.BlockSpec((D, tn), lambda j, i: (0, j0 + j))]
    args = [h, w]
    blocks = _nbytes((tm, D), BF16) + _nbytes((D, tn), w.dtype) + _nbytes((tm, tn), out_dtype)
    if rope_tables is not None:
        nt = rope_tables[0].shape[0] // tm
        for t in rope_tables:
            in_specs.append(pl.BlockSpec((tm, V7X_LANES), lambda j, i: (i % nt, 0)))
            args.append(t)
        blocks += 3 * _nbytes((tm, V7X_LANES), F32)
    n_i = M // tm
    c_in, c_out, c_shapes, c_bytes = _slab_cast_specs(cast_params, (N // tn) * n_i, lambda j, i: j * n_i + i)
    outs = pl.pallas_call(
        functools.partial(_proj_kernel, n_q_blocks=n_q_cols // tn, q_mult=q_mult, rope=rope_tables is not None,
                          n_cast=len(cast_params)),
        out_shape=[jax.ShapeDtypeStruct((M, N), out_dtype)] + c_shapes,
        grid=(N // tn, n_i),
        in_specs=in_specs + c_in,
        out_specs=[pl.BlockSpec((tm, tn), lambda j, i: (i, j))] + c_out,
        scratch_shapes=[pltpu.VMEM((D, tn), BF16)],
        compiler_params=pltpu.CompilerParams(
            dimension_semantics=("parallel", "arbitrary"),
            vmem_limit_bytes=_vmem_limit(blocks + c_bytes, _nbytes((D, tn), BF16), _nbytes((tm, tn), F32))),
        name=name,
    )(*args, *cast_params)
    return outs if cast_params else outs[0]


RET_CHUNK = 256


def _log_sigmoid(x):
    return jnp.minimum(x, 0.0) - jnp.log1p(jnp.exp(-jnp.abs(x)))


def _ret_kernel(lg_ref, q_ref, k_ref, v_ref, g_ref, *rest, has_state, emit_state, k_scale):
    rest = list(rest)
    s0f_ref = s0b_ref = sf_out = sb_out = None
    if has_state:
        s0f_ref, s0b_ref = rest[:2]
        rest = rest[2:]
    o_ref = rest[0]
    rest = rest[1:]
    if emit_state:
        sf_out, sb_out = rest[:2]
        rest = rest[2:]
    (st_scr,) = rest

    bb, T, dk = q_ref.shape
    C = min(RET_CHUNK, T)
    n = T // C

    lsig = _log_sigmoid(lg_ref[...])
    lgf = lsig[0:1, 0:1]
    lgb = lsig[1:2, 0:1]
    ri = lax.broadcasted_iota(jnp.int32, (C, C), 0)
    ci = lax.broadcasted_iota(jnp.int32, (C, C), 1)
    rel = (ri - ci).astype(F32)
    decay = jnp.where(rel > 0, jnp.exp(rel * lgf), jnp.where(rel < 0, jnp.exp(-rel * lgb), 2.0)) * k_scale
    li = lax.broadcasted_iota(jnp.int32, (C, dk), 0).astype(F32)
    xi_f = jnp.exp((li + 1.0) * lgf)
    xi_b = jnp.exp((C - li) * lgb)
    zeta_f = jnp.exp((C - 1.0 - li) * lgf) * k_scale
    zeta_b = jnp.exp(li * lgb) * k_scale
    gc_f = jnp.exp(C * lgf)
    gc_b = jnp.exp(C * lgb)

    def rows(c):
        return slice(c * C, (c + 1) * C)

    def fwd_state_used(c):
        return has_state or c > 0

    def bwd_state_used(c):
        return has_state or c < n - 1

    def kv_pair(item):
        b, t = item
        out = []
        for c, zeta in ((t, zeta_f), (n - 1 - t, zeta_b)):
            ks = (k_ref[b, rows(c), :].astype(F32) * zeta).astype(BF16)
            out.append(lax.dot_general(ks, v_ref[b, rows(c), :], (((0,), (0,)), ((), ())),
                                       preferred_element_type=F32))
        return out

    n_upd = n if emit_state else n - 1
    items = [(b, t) for b in range(bb) for t in range(n_upd)]
    kv_next = kv_pair(items[0]) if items else None
    pos = 0
    for b in range(bb):
        sf = s0f_ref[b] if has_state else None
        sb = s0b_ref[b] if has_state else None
        for t in range(n):
            cf, cb = t, n - 1 - t
            if fwd_state_used(cf):
                st_scr[b, cf, 0:dk, :] = sf.astype(BF16)
            if bwd_state_used(cb):
                st_scr[b, cb, dk:2 * dk, :] = sb.astype(BF16)
            if t < n_upd:
                kvf, kvb = kv_next
                pos += 1
                if pos < len(items):
                    kv_next = kv_pair(items[pos])
                sf = kvf if sf is None else gc_f * sf + kvf
                sb = kvb if sb is None else gc_b * sb + kvb
        if emit_state:
            sf_out[b] = sf
            sb_out[b] = sb

    def chunk_out(item):
        b, c = item
        qc = q_ref[b, rows(c), :]
        s = lax.dot_general(qc, k_ref[b, rows(c), :], (((1,), (1,)), ((), ())),
                            preferred_element_type=F32) * decay
        o = jnp.dot(s.astype(BF16), v_ref[b, rows(c), :], preferred_element_type=F32)
        qf = qc.astype(F32)
        parts = []
        if fwd_state_used(c):
            parts.append(((qf * xi_f).astype(BF16), st_scr[b, c, 0:dk, :]))
        if bwd_state_used(c):
            parts.append(((qf * xi_b).astype(BF16), st_scr[b, c, dk:2 * dk, :]))
        if len(parts) == 2:
            o = o + jnp.dot(jnp.concatenate([parts[0][0], parts[1][0]], axis=1), st_scr[b, c],
                            preferred_element_type=F32)
        elif parts:
            o = o + jnp.dot(parts[0][0], parts[0][1], preferred_element_type=F32)
        return o

    items = [(b, c) for b in range(bb) for c in range(n)]
    o_next = chunk_out(items[0])
    for pos, (b, c) in enumerate(items):
        o = o_next
        if pos + 1 < len(items):
            o_next = chunk_out(items[pos + 1])
        on = o * lax.rsqrt(jnp.mean(o * o, axis=-1, keepdims=True) + EPS)
        g = g_ref[b, rows(c), :].astype(F32)
        o_ref[b, rows(c), :] = (g * jax.nn.sigmoid(g) * on).astype(o_ref.dtype)


def _retention(pa3, lg, states, *, emit_state, dk, dv, off_q, off_k, off_v, off_g, batch_block=1):
    B, T, _ = pa3.shape
    H = N_RET_HEADS
    bb = batch_block
    has_state = states is not None
    C = min(RET_CHUNK, T)
    n = T // C
    in_specs = [pl.BlockSpec((None, 8, V7X_LANES), lambda b, h: (h, 0, 0)),
                pl.BlockSpec((bb, T, dk), lambda b, h: (b, 0, off_q // dk + h)),
                pl.BlockSpec((bb, T, dk), lambda b, h: (b, 0, off_k // dk + h)),
                pl.BlockSpec((bb, T, dv), lambda b, h: (b, 0, off_v // dv + h)),
                pl.BlockSpec((bb, T, dv), lambda b, h: (b, 0, off_g // dv + h))]
    args = [lg, pa3, pa3, pa3, pa3]
    blocks = bb * (2 * _nbytes((T, dk), BF16) + 3 * _nbytes((T, dv), BF16)) + _nbytes((8, V7X_LANES), F32)
    state_spec = pl.BlockSpec((bb, None, None, dk, dv), lambda b, h: (b, 0, h, 0, 0))
    if has_state:
        in_specs += [state_spec, state_spec]
        args += list(states)
        blocks += 2 * bb * _nbytes((dk, dv), F32)
    out_shape = [jax.ShapeDtypeStruct((B, T, H * dv), BF16)]
    out_specs = [pl.BlockSpec((bb, T, dv), lambda b, h: (b, 0, h))]
    if emit_state:
        out_shape += [jax.ShapeDtypeStruct((B, 1, H, dk, dv), F32)] * 2
        out_specs += [state_spec, state_spec]
        blocks += 2 * bb * _nbytes((dk, dv), F32)
    scratch_bytes = _nbytes((bb, n, 2 * dk, dv), BF16)
    temps = 8 * _nbytes((C, dv), F32) + 8 * _nbytes((dk, dv), F32) + 4 * _nbytes((C, C), F32)
    return pl.pallas_call(
        functools.partial(_ret_kernel, has_state=has_state, emit_state=emit_state, k_scale=float(dk) ** -0.5),
        out_shape=out_shape,
        grid=(B // bb, H),
        in_specs=in_specs,
        out_specs=out_specs,
        scratch_shapes=[pltpu.VMEM((bb, n, 2 * dk, dv), BF16)],
        compiler_params=pltpu.CompilerParams(
            dimension_semantics=("parallel", "parallel"),
            vmem_limit_bytes=_vmem_limit(blocks, scratch_bytes, temps)),
        name="retention",
    )(*args)


DIFF_SUB_ROWS = 256


def _diff_kernel(lam_ref, gs_ref, q_ref, k_ref, v_ref, *rest, has_cache, lam_init, dh, sub):
    if has_cache:
        ck_ref, cv_ref, o_ref, kall, vall = rest
    else:
        o_ref, kall, vall = rest
    T = k_ref.shape[0]
    w = 2 * dh
    heads = q_ref.shape[1] // w

    kall[0:T, :] = k_ref[...].astype(kall.dtype)
    vall[0:T, :] = v_ref[...].astype(vall.dtype)
    if has_cache:
        kall[T:, :] = ck_ref[...].astype(kall.dtype)
        vall[T:, :] = cv_ref[...].astype(vall.dtype)

    lp = lam_ref[...]
    lam = (jnp.exp(jnp.sum(lp[0:1, :] * lp[1:2, :], axis=-1, keepdims=True))
           - jnp.exp(jnp.sum(lp[2:3, :] * lp[3:4, :], axis=-1, keepdims=True)) + lam_init)
    gain = gs_ref[...] * (1.0 - lam_init)
    items = [(hh, r) for hh in range(heads) for r in range(T // sub)]

    def scores(item):
        hh, r = item
        out = []
        for m in range(2):
            cs = slice(hh * w + m * dh, hh * w + (m + 1) * dh)
            out.append(lax.dot_general(q_ref[r * sub:(r + 1) * sub, cs], kall[:, cs],
                                       (((1,), (1,)), ((), ())), preferred_element_type=F32))
        return out

    s_next = scores(items[0])
    for t, (hh, r) in enumerate(items):
        s_cur = s_next
        if t + 1 < len(items):
            s_next = scores(items[t + 1])
        probs = []
        for s in s_cur:
            e = jnp.exp2(s - jnp.max(s, axis=-1, keepdims=True))
            probs.append((e, jnp.sum(e, axis=-1, keepdims=True)))
        (e1, l1), (e2, l2) = probs
        r1 = 1.0 / l1
        a = (e1 - e2 * (lam * l1 / l2)).astype(BF16)
        o = jnp.dot(a, vall[:, hh * w:(hh + 1) * w], preferred_element_type=F32) * r1
        on = o * lax.rsqrt(jnp.mean(o * o, axis=-1, keepdims=True) + SUBLN_EPS)
        o_ref[r * sub:(r + 1) * sub, hh * w:(hh + 1) * w] = (on * gain).astype(o_ref.dtype)


def _diffattn(q_src, k_src, v_src, lam_params, g_subln, caches, *, lam_init, dh, heads_per_step):
    (qa, off_q), (ka, off_k), (va, off_v) = q_src, k_src, v_src
    B, T, _ = qa.shape
    H = N_DIFF_HEADS
    w = 2 * dh
    gw = heads_per_step * w
    sub = min(DIFF_SUB_ROWS, T)
    has_cache = caches is not None
    Tk = T + (caches[0].shape[1] if has_cache else 0)
    in_specs = [pl.BlockSpec((8, dh), lambda b, g: (0, 0)),
                pl.BlockSpec((1, w), lambda b, g: (0, 0)),
                pl.BlockSpec((None, T, gw), lambda b, g: (b, 0, off_q // gw + g)),
                pl.BlockSpec((None, T, gw), lambda b, g: (b, 0, off_k // gw + g)),
                pl.BlockSpec((None, T, gw), lambda b, g: (b, 0, off_v // gw + g))]
    args = [lam_params, g_subln, qa, ka, va]
    blocks = 2 * _nbytes((T, gw), BF16) + _nbytes((T, gw), ka.dtype) + _nbytes((T, gw), va.dtype)
    if has_cache:
        P = caches[0].shape[1]
        cspec = pl.BlockSpec((None, P, gw), lambda b, g: (b, 0, g))
        in_specs += [cspec, cspec]
        args += list(caches)
        blocks += 2 * _nbytes((P, gw), F32)
    temps = 10 * _nbytes((sub, Tk), F32)
    return pl.pallas_call(
        functools.partial(_diff_kernel, has_cache=has_cache, lam_init=lam_init, dh=dh, sub=sub),
        out_shape=jax.ShapeDtypeStruct((B, T, H * w), BF16),
        grid=(B, H // heads_per_step),
        in_specs=in_specs,
        out_specs=pl.BlockSpec((None, T, gw), lambda b, g: (b, 0, g)),
        scratch_shapes=[pltpu.VMEM((Tk, gw), BF16), pltpu.VMEM((Tk, gw), BF16)],
        compiler_params=pltpu.CompilerParams(
            dimension_semantics=("parallel", "parallel"),
            vmem_limit_bytes=_vmem_limit(blocks, 2 * _nbytes((Tk, gw), BF16), temps)),
        name="diffattn",
    )(*args)


def _merge_a_kernel(r_ref, d_ref, wr_ref, wd_ref, gr_ref, gd_ref, *rest):
    n_cast = len(rest) // 2
    cast_in, o_ref, cast_out = rest[:n_cast], rest[n_cast], rest[n_cast + 1:]
    for src, dst in zip(cast_in, cast_out):
        dst[...] = src[...].astype(dst.dtype)
    rb = jnp.dot(r_ref[...], wr_ref[...], preferred_element_type=F32)
    db = jnp.dot(d_ref[...], wd_ref[...], preferred_element_type=F32)
    gr = jax.nn.sigmoid(gr_ref[...].astype(F32))
    gd = jax.nn.sigmoid(gd_ref[...].astype(F32))
    o_ref[...] = (gr * rb + gd * db).astype(o_ref.dtype)


def _merge_a(ret_g, diff_n, w_ret_o, w_diff_o, gates, *, off_gr, off_gd, cast_params=(), tm=1024, tn=512):
    M, Kr = ret_g.shape
    Kd = diff_n.shape[1]
    D = w_ret_o.shape[1]
    tm = min(tm, M)
    n_j = D // tn
    blocks = (_nbytes((tm, Kr), BF16) + _nbytes((tm, Kd), BF16) + _nbytes((Kr, tn), BF16)
              + _nbytes((Kd, tn), BF16) + 3 * _nbytes((tm, tn), BF16))
    c_in, c_out, c_shapes, c_bytes = _slab_cast_specs(cast_params, (M // tm) * n_j, lambda i, j: i * n_j + j)
    outs = pl.pallas_call(
        _merge_a_kernel,
        out_shape=[jax.ShapeDtypeStruct((M, D), BF16)] + c_shapes,
        grid=(M // tm, n_j),
        in_specs=[pl.BlockSpec((tm, Kr), lambda i, j: (i, 0)),
                  pl.BlockSpec((tm, Kd), lambda i, j: (i, 0)),
                  pl.BlockSpec((Kr, tn), lambda i, j: (0, j)),
                  pl.BlockSpec((Kd, tn), lambda i, j: (0, j)),
                  pl.BlockSpec((tm, tn), lambda i, j: (i, off_gr // tn + j)),
                  pl.BlockSpec((tm, tn), lambda i, j: (i, off_gd // tn + j))] + c_in,
        out_specs=[pl.BlockSpec((tm, tn), lambda i, j: (i, j))] + c_out,
        compiler_params=pltpu.CompilerParams(
            dimension_semantics=("parallel", "arbitrary"),
            vmem_limit_bytes=_vmem_limit(blocks + c_bytes, temp_bytes=4 * _nbytes((tm, tn), F32))),
        name="merge_a",
    )(ret_g, diff_n, w_ret_o, w_diff_o, gates, gates, *cast_params)
    return outs if cast_params else outs[0]


MERGE_SUB_ROWS = 256


def _merge_b_kernel(m_ref, w_ref, x_ref, mod_ref, gpost_ref, gpre_ref, x1_ref, h2_ref):
    gate_post = mod_ref[2:3, :] * gpost_ref[...]
    a2 = gpre_ref[...] * (1.0 + mod_ref[4:5, :])
    shift2 = mod_ref[3:4, :]
    sub = MERGE_SUB_ROWS
    n_sub = x_ref.shape[0] // sub

    def out_proj(k):
        return jnp.dot(m_ref[k * sub:(k + 1) * sub, :], w_ref[...], preferred_element_type=F32)

    y_next = out_proj(0)
    for k in range(n_sub):
        y_all = y_next
        if k + 1 < n_sub:
            y_next = out_proj(k + 1)
        for c in range(sub // ROW_CHUNK):
            sl = slice(k * sub + c * ROW_CHUNK, k * sub + (c + 1) * ROW_CHUNK)
            y = y_all[c * ROW_CHUNK:(c + 1) * ROW_CHUNK, :]
            x1 = x_ref[sl, :] + y * lax.rsqrt(jnp.mean(y * y, axis=-1, keepdims=True) + EPS) * gate_post
            x1_ref[sl, :] = x1
            h2_ref[sl, :] = (x1 * lax.rsqrt(jnp.mean(x1 * x1, axis=-1, keepdims=True) + EPS) * a2
                             + shift2).astype(h2_ref.dtype)


def _merge_b(merged, w_out, x2, mod, g_post, g_pre, *, rows_per_group, tm=512):
    M, D = x2.shape
    tm = min(tm, M)
    blocks = (_nbytes((tm, D), BF16) * 2 + _nbytes((D, D), BF16) + 2 * _nbytes((tm, D), F32)
              + _nbytes((N_MOD + 2, D), F32))
    return pl.pallas_call(
        _merge_b_kernel,
        out_shape=[jax.ShapeDtypeStruct((M, D), F32), jax.ShapeDtypeStruct((M, D), BF16)],
        grid=(M // tm,),
        in_specs=[pl.BlockSpec((tm, D), lambda i: (i, 0)),
                  pl.BlockSpec((D, D), lambda i: (0, 0)),
                  pl.BlockSpec((tm, D), lambda i: (i, 0)),
                  pl.BlockSpec((None, N_MOD, D), _mod_map(mod.shape[0], rows_per_group, tm)),
                  pl.BlockSpec((1, D), lambda i: (0, 0)),
                  pl.BlockSpec((1, D), lambda i: (0, 0))],
        out_specs=[pl.BlockSpec((tm, D), lambda i: (i, 0)),
                   pl.BlockSpec((tm, D), lambda i: (i, 0))],
        compiler_params=pltpu.CompilerParams(
            dimension_semantics=("parallel",),
            vmem_limit_bytes=_vmem_limit(blocks, temp_bytes=2 * _nbytes((tm, D), F32))),
        name="merge_b",
    )(merged, w_out, x2, mod, g_post, g_pre)


def _mlp_kernel(h_ref, wu_ref, wd_ref, x_ref, mod_ref, g_ref, o_ref, acc_ref):
    j = pl.program_id(1)

    @pl.when(j == 0)
    def _():
        acc_ref[...] = jnp.zeros_like(acc_ref)

    u = jnp.maximum(jnp.dot(h_ref[...], wu_ref[...], preferred_element_type=F32), 0.0)
    acc_ref[...] += jnp.dot((u * u).astype(BF16), wd_ref[...], preferred_element_type=F32)

    @pl.when(j == pl.num_programs(1) - 1)
    def _():
        gate_post = mod_ref[5:6, :] * g_ref[...]
        n = x_ref.shape[0] // ROW_CHUNK

        def body(r, carry):
            sl = pl.ds(pl.multiple_of(r * ROW_CHUNK, ROW_CHUNK), ROW_CHUNK)
            y = acc_ref[sl, :]
            o_ref[sl, :] = x_ref[sl, :] + y * lax.rsqrt(jnp.mean(y * y, axis=-1, keepdims=True) + EPS) * gate_post
            return carry

        lax.fori_loop(0, n, body, 0, unroll=NORM_UNROLL)


def _mlp(h2, w_up, w_down, x1, mod, g_post, *, rows_per_group, tm=512, tf=1024):
    M, D = x1.shape
    Fh = w_up.shape[1]
    tm = min(tm, M)
    blocks = (_nbytes((tm, D), BF16) + 2 * _nbytes((D, tf), BF16) + 2 * _nbytes((tm, D), F32)
              + _nbytes((N_MOD + 1, D), F32))
    return pl.pallas_call(
        _mlp_kernel,
        out_shape=jax.ShapeDtypeStruct((M, D), F32),
        grid=(M // tm, Fh // tf),
        in_specs=[pl.BlockSpec((tm, D), lambda i, j: (i, 0)),
                  pl.BlockSpec((D, tf), lambda i, j: (0, j)),
                  pl.BlockSpec((tf, D), lambda i, j: (j, 0)),
                  pl.BlockSpec((tm, D), lambda i, j: (i, 0)),
                  pl.BlockSpec((None, N_MOD, D), _mod_map(mod.shape[0], rows_per_group, tm)),
                  pl.BlockSpec((1, D), lambda i, j: (0, 0))],
        out_specs=pl.BlockSpec((tm, D), lambda i, j: (i, 0)),
        scratch_shapes=[pltpu.VMEM((tm, D), F32)],
        compiler_params=pltpu.CompilerParams(
            dimension_semantics=("parallel", "arbitrary"),
            vmem_limit_bytes=_vmem_limit(blocks, _nbytes((tm, D), F32),
                                         2 * _nbytes((tm, tf), F32) + _nbytes((tm, D), F32))),
        name="mlp",
    )(h2, w_up, w_down, x1, mod, g_post)


def _rope_tables(T, rope_half):
    rows = T // GRID_W
    row = jnp.repeat(jnp.arange(rows, dtype=F32), GRID_W)
    col = jnp.tile(jnp.arange(GRID_W, dtype=F32), rows)
    inv = ROPE_BASE ** (-jnp.arange(0, rope_half, 2, dtype=F32) / rope_half)
    ar, ac = row[:, None] * inv, col[:, None] * inv
    z = jnp.zeros_like(ar)
    cos = jnp.concatenate([jnp.cos(ar), jnp.cos(ar), jnp.cos(ac), jnp.cos(ac)], axis=-1)
    sin_next = jnp.concatenate([-jnp.sin(ar), z, -jnp.sin(ac), z], axis=-1)
    sin_prev = jnp.concatenate([z, jnp.sin(ar), z, jnp.sin(ac)], axis=-1)
    return cos, sin_next, sin_prev


def kernel(x_prompt, x_sample, cache_k, cache_v, state_ret_fwd, state_ret_bwd, c, c_ctx, w_ada, b_ada, g_mix_pre, g_mix_post, g_mlp_pre, g_mlp_post, w_in, ret_gamma_logit_fwd, ret_gamma_logit_bwd, w_ret_o, lambda_q1, lambda_k1, lambda_q2, lambda_k2, g_diff_subln, w_diff_o, w_out, w_mlp_up, w_mlp_down):
    Bp, Tp, D = x_prompt.shape
    Bs, Ts, _ = x_sample.shape
    depth = w_in.shape[0]
    assert depth == 1
    l = 0
    H = N_RET_HEADS
    dk = D // H
    dv = 2 * dk
    dh = D // N_DIFF_HEADS // 2
    ret_qk_w, ret_v_w, diff_w = H * dk, H * dv, N_DIFF_HEADS * 2 * dh
    o_dq = 2 * ret_qk_w + 2 * ret_v_w
    o_dk, o_dv, o_gate = o_dq + diff_w, o_dq + 2 * diff_w, o_dq + 3 * diff_w
    in_w = o_gate + 2 * D
    assert w_in.shape[2] == in_w
    lam_init = 0.8 - 0.6 * math.exp(-0.3 * l)
    q_mult = float(dh) ** -0.5 * math.log2(math.e)

    w_in_l = w_in[l]
    g_mix_pre_l = g_mix_pre[l].reshape(1, D)
    g_mix_post_l = g_mix_post[l].reshape(1, D)
    g_mlp_pre_l = g_mlp_pre[l].reshape(1, D)
    g_mlp_post_l = g_mlp_post[l].reshape(1, D)
    g_subln_l = g_diff_subln[l].reshape(1, 2 * dh)
    lam_params = jnp.concatenate(
        [jnp.stack([lambda_q1[l], lambda_k1[l], lambda_q2[l], lambda_k2[l]]), jnp.zeros((4, dh), F32)], axis=0)
    lg = jnp.stack([ret_gamma_logit_fwd[l], ret_gamma_logit_bwd[l]], axis=1)
    lg = jnp.concatenate([lg, jnp.zeros((H, 6), F32)], axis=1)
    lg = jnp.broadcast_to(lg[:, :, None], (H, 8, V7X_LANES))

    n_cond = 16
    cond = jnp.concatenate([c, jnp.broadcast_to(c_ctx[None, :], (n_cond - Bs, D))], axis=0)
    mod = _adaln(cond, w_ada[l], b_ada[l].reshape(1, N_MOD * D)).reshape(n_cond, N_MOD, D)

    ret_offs = dict(off_q=0, off_k=ret_qk_w, off_v=2 * ret_qk_w, off_g=2 * ret_qk_w + ret_v_w)

    xs2 = x_sample.reshape(Bs * Ts, D)
    hs = _prenorm(xs2, mod, g_mix_pre_l, rows_per_group=Ts)
    sa = _proj(hs, w_in_l, (0, o_dq), out_dtype=BF16, tm=2048, name="proj_ret").reshape(Bs, Ts, -1)
    sqk, w_ret_o_b, w_diff_o_b, w_out_b = _proj(
        hs, w_in_l, (o_dq, o_dv), out_dtype=BF16, n_q_cols=diff_w, q_mult=q_mult,
        rope_tables=_rope_tables(Ts, dh // 2), cast_params=(w_ret_o[l], w_diff_o[l], w_out[l]), name="proj_qk_rope")
    sqk = sqk.reshape(Bs, Ts, 2 * diff_w)
    svg = _proj(hs, w_in_l, (o_dv, in_w), out_dtype=BF16, tm=2048, name="proj_vg")
    (ret_g,) = _retention(sa, lg, (state_ret_fwd, state_ret_bwd), emit_state=False, dk=dk, dv=dv, **ret_offs)
    caches = (cache_k[:, l].reshape(Bs, -1, diff_w), cache_v[:, l].reshape(Bs, -1, diff_w))
    diff_n = _diffattn((sqk, 0), (sqk, diff_w), (svg.reshape(Bs, Ts, -1), 0),
                       lam_params, g_subln_l, caches, lam_init=lam_init, dh=dh, heads_per_step=1)
    merged, w_up_b, w_down_b = _merge_a(
        ret_g.reshape(Bs * Ts, ret_v_w), diff_n.reshape(Bs * Ts, diff_w), w_ret_o_b, w_diff_o_b, svg,
        off_gr=diff_w, off_gd=diff_w + D, cast_params=(w_mlp_up[l], w_mlp_down[l]))

    def tail(merged, x2, B, T, rows_per_group):
        x1, h2 = _merge_b(merged, w_out_b, x2, mod, g_mix_post_l, g_mlp_pre_l, rows_per_group=rows_per_group)
        y = _mlp(h2, w_up_b, w_down_b, x1, mod, g_mlp_post_l, rows_per_group=rows_per_group)
        return y.reshape(B, T, D)

    y_sample = tail(merged, xs2, Bs, Ts, Ts)

    xp2 = x_prompt.reshape(Bp * Tp, D)
    hp = _prenorm(xp2, mod, g_mix_pre_l, rows_per_group=None)
    pa = _proj(hp, w_in_l, (0, o_dq), out_dtype=BF16, name="proj_ret").reshape(Bp, Tp, -1)
    pq = _proj(hp, w_in_l, (o_dq, o_dk), out_dtype=BF16, n_q_cols=diff_w, q_mult=q_mult,
               name="proj_q").reshape(Bp, Tp, diff_w)
    pk = _proj(hp, w_in_l, (o_dk, o_dv), out_dtype=F32, name="proj_k")
    pv = _proj(hp, w_in_l, (o_dv, o_gate), out_dtype=F32, name="proj_v")
    pg = _proj(hp, w_in_l, (o_gate, in_w), out_dtype=BF16, name="proj_gates")
    ret_g, new_state_fwd, new_state_bwd = _retention(pa, lg, None, emit_state=True, dk=dk, dv=dv,
                                                     batch_block=4, **ret_offs)
    diff_n = _diffattn((pq, 0), (pk.reshape(Bp, Tp, diff_w), 0), (pv.reshape(Bp, Tp, diff_w), 0),
                       lam_params, g_subln_l, None, lam_init=lam_init, dh=dh, heads_per_step=N_DIFF_HEADS)
    merged = _merge_a(ret_g.reshape(Bp * Tp, ret_v_w), diff_n.reshape(Bp * Tp, diff_w), w_ret_o_b, w_diff_o_b, pg,
                      off_gr=0, off_gd=D)
    y_prompt = tail(merged, xp2, Bp, Tp, None)
    new_cache_k = pk.reshape(Bp, 1, Tp, N_DIFF_HEADS, 2, dh)
    new_cache_v = pv.reshape(Bp, 1, Tp, N_DIFF_HEADS, 2 * dh)

    return (y_prompt, y_sample, new_cache_k, new_cache_v, new_state_fwd, new_state_bwd)
```

```python
import functools
import math

import jax
import jax.numpy as jnp
from jax import lax
from jax.experimental import pallas as pl
from jax.experimental.pallas import tpu as pltpu

F32 = jnp.float32
BF16 = jnp.bfloat16

N_RET_HEADS = 8
N_DIFF_HEADS = 8
N_MOD = 6
GRID_W = 64
ROPE_BASE = 10000.0
EPS = 1e-6
SUBLN_EPS = 1e-5

V7X_LANES = 128
V7X_VMEM_BYTES = 64 * 1024 * 1024
VMEM_RESERVE_BYTES = 8 * 1024 * 1024


def _vmem_limit(block_bytes, scratch_bytes=0, temp_bytes=0):
    want = 2 * block_bytes + scratch_bytes + temp_bytes + VMEM_RESERVE_BYTES
    return int(min(want, V7X_VMEM_BYTES - VMEM_RESERVE_BYTES // 2))


def _nbytes(shape, dtype):
    return math.prod(shape) * jnp.dtype(dtype).itemsize


def _mod_map(n_groups, rows_per_group, tm):
    if rows_per_group is None:
        return lambda i, *_: (n_groups - 1, 0, 0)
    return lambda i, *_: ((i * tm) // rows_per_group, 0, 0)


def _adaln_kernel(c_ref, w_ref, b_ref, o_ref):
    c = c_ref[...]
    s = (c * jax.nn.sigmoid(c)).astype(BF16)
    o_ref[...] = jnp.dot(s, w_ref[...].astype(BF16), preferred_element_type=F32) + b_ref[...]


def _adaln(cond, w_ada, b_ada, tn=512):
    R, D = cond.shape
    N = w_ada.shape[1]
    blocks = _nbytes((R, D), F32) + _nbytes((D, tn), F32) + _nbytes((1, tn), F32) + _nbytes((R, tn), F32)
    return pl.pallas_call(
        _adaln_kernel,
        out_shape=jax.ShapeDtypeStruct((R, N), F32),
        grid=(N // tn,),
        in_specs=[pl.BlockSpec((R, D), lambda j: (0, 0)),
                  pl.BlockSpec((D, tn), lambda j: (0, j)),
                  pl.BlockSpec((1, tn), lambda j: (0, j))],
        out_specs=pl.BlockSpec((R, tn), lambda j: (0, j)),
        compiler_params=pltpu.CompilerParams(
            dimension_semantics=("arbitrary",),
            vmem_limit_bytes=_vmem_limit(blocks, temp_bytes=_nbytes((D, tn), BF16))),
        name="adaln",
    )(cond, w_ada, b_ada)


ROW_CHUNK = 64
NORM_UNROLL = 4


def _norm_modulate(x_ref, gain, shift, scale, h_ref):
    a = gain * (1.0 + scale)
    n = x_ref.shape[0] // ROW_CHUNK

    def body(r, carry):
        sl = pl.ds(pl.multiple_of(r * ROW_CHUNK, ROW_CHUNK), ROW_CHUNK)
        x = x_ref[sl, :]
        ms = jnp.mean(x * x, axis=-1, keepdims=True)
        h_ref[sl, :] = (x * lax.rsqrt(ms + EPS) * a + shift).astype(h_ref.dtype)
        return carry

    lax.fori_loop(0, n, body, 0, unroll=NORM_UNROLL)


def _prenorm_kernel(x_ref, mod_ref, g_ref, h_ref):
    _norm_modulate(x_ref, g_ref[...], mod_ref[0:1, :], mod_ref[1:2, :], h_ref)


def _prenorm(x2, mod, gain, *, rows_per_group, tm=1024):
    M, D = x2.shape
    blocks = _nbytes((tm, D), F32) + _nbytes((tm, D), BF16) + _nbytes((N_MOD + 1, D), F32)
    return pl.pallas_call(
        _prenorm_kernel,
        out_shape=jax.ShapeDtypeStruct((M, D), BF16),
        grid=(M // tm,),
        in_specs=[pl.BlockSpec((tm, D), lambda i: (i, 0)),
                  pl.BlockSpec((None, N_MOD, D), _mod_map(mod.shape[0], rows_per_group, tm)),
                  pl.BlockSpec((1, D), lambda i: (0, 0))],
        out_specs=pl.BlockSpec((tm, D), lambda i: (i, 0)),
        compiler_params=pltpu.CompilerParams(
            dimension_semantics=("parallel",),
            vmem_limit_bytes=_vmem_limit(blocks, temp_bytes=4 * _nbytes((ROW_CHUNK, D), F32))),
        name="prenorm",
    )(x2, mod, gain)


CAST_ROWS = 256


def _slab_cast_specs(params, n_steps, step_index):
    in_specs, out_specs, out_shapes, nbytes = [], [], [], 0
    for p in params:
        rows, width = p.shape[0] // n_steps, p.shape[1]
        assert rows * n_steps == p.shape[0] and rows % 16 == 0
        spec = pl.BlockSpec((rows, width), lambda *ids: (step_index(*ids), 0))
        in_specs.append(spec)
        out_specs.append(spec)
        out_shapes.append(jax.ShapeDtypeStruct(p.shape, BF16))
        nbytes += _nbytes((rows, width), F32) + _nbytes((rows, width), BF16)
    return in_specs, out_specs, out_shapes, nbytes


def _proj_kernel(h_ref, w_ref, *rest, n_q_blocks, q_mult, rope, n_cast):
    rest = list(rest)
    if rope:
        cos_ref, sa_ref, sb_ref = rest[:3]
        rest = rest[3:]
    cast_in, o_ref, cast_out, wb_ref = rest[:n_cast], rest[n_cast], rest[n_cast + 1:2 * n_cast + 1], rest[-1]
    for src, dst in zip(cast_in, cast_out):
        dst[...] = src[...].astype(dst.dtype)

    @pl.when(pl.program_id(1) == 0)
    def _():
        def body(r, carry):
            sl = pl.ds(pl.multiple_of(r * CAST_ROWS, CAST_ROWS), CAST_ROWS)
            wb_ref[sl, :] = w_ref[sl, :].astype(wb_ref.dtype)
            return carry

        lax.fori_loop(0, w_ref.shape[0] // CAST_ROWS, body, 0)

    acc = jnp.dot(h_ref[...], wb_ref[...], preferred_element_type=F32)
    mult = None
    if n_q_blocks:
        mult = jnp.where(pl.program_id(0) < n_q_blocks, q_mult, 1.0).astype(F32)
    if rope:
        cos, sa, sb = cos_ref[...] * mult, sa_ref[...] * mult, sb_ref[...] * mult
        for c in range(acc.shape[1] // V7X_LANES):
            cs = slice(c * V7X_LANES, (c + 1) * V7X_LANES)
            xc = acc[:, cs]
            y = xc * cos + pltpu.roll(xc, 96, 1) * sa + pltpu.roll(xc, 32, 1) * sb
            o_ref[:, cs] = y.astype(o_ref.dtype)
    elif mult is not None:
        o_ref[...] = (acc * mult).astype(o_ref.dtype)
    else:
        o_ref[...] = acc.astype(o_ref.dtype)


def _proj(h, w, cols, *, out_dtype, n_q_cols=0, q_mult=1.0, rope_tables=None, cast_params=(),
          tm=1024, tn=1024, name="proj"):
    M, D = h.shape
    N = cols[1] - cols[0]
    tm = min(tm, M)
    tn = min(tn, N)
    j0 = cols[0] // tn
    in_specs = [pl.BlockSpec((tm, D), lambda j, i: (i, 0)),
                pl.BlockSpec((D, tn), lambda j, i: (0, j0 + j))]
    args = [h, w]
    blocks = _nbytes((tm, D), BF16) + _nbytes((D, tn), w.dtype) + _nbytes((tm, tn), out_dtype)
    if rope_tables is not None:
        nt = rope_tables[0].shape[0] // tm
        for t in rope_tables:
            in_specs.append(pl.BlockSpec((tm, V7X_LANES), lambda j, i: (i % nt, 0)))
            args.append(t)
        blocks += 3 * _nbytes((tm, V7X_LANES), F32)
    n_i = M // tm
    c_in, c_out, c_shapes, c_bytes = _slab_cast_specs(cast_params, (N // tn) * n_i, lambda j, i: j * n_i + i)
    outs = pl.pallas_call(
        functools.partial(_proj_kernel, n_q_blocks=n_q_cols // tn, q_mult=q_mult, rope=rope_tables is not None,
                          n_cast=len(cast_params)),
        out_shape=[jax.ShapeDtypeStruct((M, N), out_dtype)] + c_shapes,
        grid=(N // tn, n_i),
        in_specs=in_specs + c_in,
        out_specs=[pl.BlockSpec((tm, tn), lambda j, i: (i, j))] + c_out,
        scratch_shapes=[pltpu.VMEM((D, tn), BF16)],
        compiler_params=pltpu.CompilerParams(
            dimension_semantics=("parallel", "arbitrary"),
            vmem_limit_bytes=_vmem_limit(blocks + c_bytes, _nbytes((D, tn), BF16), _nbytes((tm, tn), F32))),
        name=name,
    )(*args, *cast_params)
    return outs if cast_params else outs[0]


RET_CHUNK = 256


def _log_sigmoid(x):
    return jnp.minimum(x, 0.0) - jnp.log1p(jnp.exp(-jnp.abs(x)))


def _ret_kernel(lg_ref, q_ref, k_ref, v_ref, g_ref, *rest, has_state, emit_state, k_scale):
    rest = list(rest)
    s0f_ref = s0b_ref = sf_out = sb_out = None
    if has_state:
        s0f_ref, s0b_ref = rest[:2]
        rest = rest[2:]
    o_ref = rest[0]
    rest = rest[1:]
    if emit_state:
        sf_out, sb_out = rest[:2]
        rest = rest[2:]
    (st_scr,) = rest

    bb, T, dk = q_ref.shape
    C = min(RET_CHUNK, T)
    n = T // C

    lsig = _log_sigmoid(lg_ref[...])
    lgf = lsig[0:1, 0:1]
    lgb = lsig[1:2, 0:1]
    ri = lax.broadcasted_iota(jnp.int32, (C, C), 0)
    ci = lax.broadcasted_iota(jnp.int32, (C, C), 1)
    rel = (ri - ci).astype(F32)
    decay = jnp.where(rel > 0, jnp.exp(rel * lgf), jnp.where(rel < 0, jnp.exp(-rel * lgb), 2.0)) * k_scale
    li = lax.broadcasted_iota(jnp.int32, (C, dk), 0).astype(F32)
    xi_f = jnp.exp((li + 1.0) * lgf)
    xi_b = jnp.exp((C - li) * lgb)
    zeta_f = jnp.exp((C - 1.0 - li) * lgf) * k_scale
    zeta_b = jnp.exp(li * lgb) * k_scale
    gc_f = jnp.exp(C * lgf)
    gc_b = jnp.exp(C * lgb)

    def rows(c):
        return slice(c * C, (c + 1) * C)

    def fwd_state_used(c):
        return has_state or c > 0

    def bwd_state_used(c):
        return has_state or c < n - 1

    def kv_pair(item):
        b, t = item
        out = []
        for c, zeta in ((t, zeta_f), (n - 1 - t, zeta_b)):
            ks = (k_ref[b, rows(c), :].astype(F32) * zeta).astype(BF16)
            out.append(lax.dot_general(ks, v_ref[b, rows(c), :], (((0,), (0,)), ((), ())),
                                       preferred_element_type=F32))
        return out

    n_upd = n if emit_state else n - 1
    items = [(b, t) for b in range(bb) for t in range(n_upd)]
    kv_next = kv_pair(items[0]) if items else None
    pos = 0
    for b in range(bb):
        sf = s0f_ref[b] if has_state else None
        sb = s0b_ref[b] if has_state else None
        for t in range(n):
            cf, cb = t, n - 1 - t
            if fwd_state_used(cf):
                st_scr[b, cf, 0:dk, :] = sf.astype(BF16)
            if bwd_state_used(cb):
                st_scr[b, cb, dk:2 * dk, :] = sb.astype(BF16)
            if t < n_upd:
                kvf, kvb = kv_next
                pos += 1
                if pos < len(items):
                    kv_next = kv_pair(items[pos])
                sf = kvf if sf is None else gc_f * sf + kvf
                sb = kvb if sb is None else gc_b * sb + kvb
        if emit_state:
            sf_out[b] = sf
            sb_out[b] = sb

    def chunk_out(item):
        b, c = item
        qc = q_ref[b, rows(c), :]
        s = lax.dot_general(qc, k_ref[b, rows(c), :], (((1,), (1,)), ((), ())),
                            preferred_element_type=F32) * decay
        o = jnp.dot(s.astype(BF16), v_ref[b, rows(c), :], preferred_element_type=F32)
        qf = qc.astype(F32)
        parts = []
        if fwd_state_used(c):
            parts.append(((qf * xi_f).astype(BF16), st_scr[b, c, 0:dk, :]))
        if bwd_state_used(c):
            parts.append(((qf * xi_b).astype(BF16), st_scr[b, c, dk:2 * dk, :]))
        if len(parts) == 2:
            o = o + jnp.dot(jnp.concatenate([parts[0][0], parts[1][0]], axis=1), st_scr[b, c],
                            preferred_element_type=F32)
        elif parts:
            o = o + jnp.dot(parts[0][0], parts[0][1], preferred_element_type=F32)
        return o

    items = [(b, c) for b in range(bb) for c in range(n)]
    o_next = chunk_out(items[0])
    for pos, (b, c) in enumerate(items):
        o = o_next
        if pos + 1 < len(items):
            o_next = chunk_out(items[pos + 1])
        on = o * lax.rsqrt(jnp.mean(o * o, axis=-1, keepdims=True) + EPS)
        hg = g_ref[b, rows(c), :] * 0.5
        o_ref[b, rows(c), :] = (hg + hg * jnp.tanh(hg)) * on.astype(o_ref.dtype)


def _retention(pa3, lg, states, *, emit_state, dk, dv, off_q, off_k, off_v, off_g, batch_block=1):
    B, T, _ = pa3.shape
    H = N_RET_HEADS
    bb = batch_block
    has_state = states is not None
    C = min(RET_CHUNK, T)
    n = T // C
    in_specs = [pl.BlockSpec((None, 8, V7X_LANES), lambda b, h: (h, 0, 0)),
                pl.BlockSpec((bb, T, dk), lambda b, h: (b, 0, off_q // dk + h)),
                pl.BlockSpec((bb, T, dk), lambda b, h: (b, 0, off_k // dk + h)),
                pl.BlockSpec((bb, T, dv), lambda b, h: (b, 0, off_v // dv + h)),
                pl.BlockSpec((bb, T, dv), lambda b, h: (b, 0, off_g // dv + h))]
    args = [lg, pa3, pa3, pa3, pa3]
    blocks = bb * (2 * _nbytes((T, dk), BF16) + 3 * _nbytes((T, dv), BF16)) + _nbytes((8, V7X_LANES), F32)
    state_spec = pl.BlockSpec((bb, None, None, dk, dv), lambda b, h: (b, 0, h, 0, 0))
    if has_state:
        in_specs += [state_spec, state_spec]
        args += list(states)
        blocks += 2 * bb * _nbytes((dk, dv), F32)
    out_shape = [jax.ShapeDtypeStruct((B, T, H * dv), BF16)]
    out_specs = [pl.BlockSpec((bb, T, dv), lambda b, h: (b, 0, h))]
    if emit_state:
        out_shape += [jax.ShapeDtypeStruct((B, 1, H, dk, dv), F32)] * 2
        out_specs += [state_spec, state_spec]
        blocks += 2 * bb * _nbytes((dk, dv), F32)
    scratch_bytes = _nbytes((bb, n, 2 * dk, dv), BF16)
    temps = 8 * _nbytes((C, dv), F32) + 8 * _nbytes((dk, dv), F32) + 4 * _nbytes((C, C), F32)
    return pl.pallas_call(
        functools.partial(_ret_kernel, has_state=has_state, emit_state=emit_state, k_scale=float(dk) ** -0.5),
        out_shape=out_shape,
        grid=(B // bb, H),
        in_specs=in_specs,
        out_specs=out_specs,
        scratch_shapes=[pltpu.VMEM((bb, n, 2 * dk, dv), BF16)],
        compiler_params=pltpu.CompilerParams(
            dimension_semantics=("parallel", "parallel"),
            vmem_limit_bytes=_vmem_limit(blocks, scratch_bytes, temps)),
        name="retention",
    )(*args)


DIFF_SUB_ROWS = 256


def _diff_kernel(lam_ref, gs_ref, q_ref, k_ref, v_ref, *rest, has_cache, lam_init, dh, sub):
    if has_cache:
        ck_ref, cv_ref, o_ref, kall, vall = rest
    else:
        o_ref, kall, vall = rest
    T = k_ref.shape[0]
    w = 2 * dh
    heads = q_ref.shape[1] // w

    kall[0:T, :] = k_ref[...].astype(kall.dtype)
    vall[0:T, :] = v_ref[...].astype(vall.dtype)
    if has_cache:
        kall[T:, :] = ck_ref[...].astype(kall.dtype)
        vall[T:, :] = cv_ref[...].astype(vall.dtype)

    lp = lam_ref[...]
    lam = (jnp.exp(jnp.sum(lp[0:1, :] * lp[1:2, :], axis=-1, keepdims=True))
           - jnp.exp(jnp.sum(lp[2:3, :] * lp[3:4, :], axis=-1, keepdims=True)) + lam_init)
    gain = gs_ref[...] * (1.0 - lam_init)
    items = [(hh, r) for hh in range(heads) for r in range(T // sub)]

    def scores(item):
        hh, r = item
        out = []
        for m in range(2):
            cs = slice(hh * w + m * dh, hh * w + (m + 1) * dh)
            out.append(lax.dot_general(q_ref[r * sub:(r + 1) * sub, cs], kall[:, cs],
                                       (((1,), (1,)), ((), ())), preferred_element_type=F32))
        return out

    s_next = scores(items[0])
    for t, (hh, r) in enumerate(items):
        s_cur = s_next
        if t + 1 < len(items):
            s_next = scores(items[t + 1])
        probs = []
        for s in s_cur:
            e = jnp.exp2(s - jnp.max(s, axis=-1, keepdims=True))
            probs.append((e, jnp.sum(e, axis=-1, keepdims=True)))
        (e1, l1), (e2, l2) = probs
        r1 = 1.0 / l1
        a = (e1 - e2 * (lam * l1 / l2)).astype(BF16)
        o = jnp.dot(a, vall[:, hh * w:(hh + 1) * w], preferred_element_type=F32) * r1
        on = o * lax.rsqrt(jnp.mean(o * o, axis=-1, keepdims=True) + SUBLN_EPS)
        o_ref[r * sub:(r + 1) * sub, hh * w:(hh + 1) * w] = (on * gain).astype(o_ref.dtype)


def _diffattn(q_src, k_src, v_src, lam_params, g_subln, caches, *, lam_init, dh, heads_per_step):
    (qa, off_q), (ka, off_k), (va, off_v) = q_src, k_src, v_src
    B, T, _ = qa.shape
    H = N_DIFF_HEADS
    w = 2 * dh
    gw = heads_per_step * w
    sub = min(DIFF_SUB_ROWS, T)
    has_cache = caches is not None
    Tk = T + (caches[0].shape[1] if has_cache else 0)
    in_specs = [pl.BlockSpec((8, dh), lambda b, g: (0, 0)),
                pl.BlockSpec((1, w), lambda b, g: (0, 0)),
                pl.BlockSpec((None, T, gw), lambda b, g: (b, 0, off_q // gw + g)),
                pl.BlockSpec((None, T, gw), lambda b, g: (b, 0, off_k // gw + g)),
                pl.BlockSpec((None, T, gw), lambda b, g: (b, 0, off_v // gw + g))]
    args = [lam_params, g_subln, qa, ka, va]
    blocks = 2 * _nbytes((T, gw), BF16) + _nbytes((T, gw), ka.dtype) + _nbytes((T, gw), va.dtype)
    if has_cache:
        P = caches[0].shape[1]
        cspec = pl.BlockSpec((None, P, gw), lambda b, g: (b, 0, g))
        in_specs += [cspec, cspec]
        args += list(caches)
        blocks += 2 * _nbytes((P, gw), F32)
    temps = 10 * _nbytes((sub, Tk), F32)
    scratch_rows = [Tk, Tk]
    return pl.pallas_call(
        functools.partial(_diff_kernel, has_cache=has_cache, lam_init=lam_init, dh=dh, sub=sub),
        out_shape=jax.ShapeDtypeStruct((B, T, H * w), BF16),
        grid=(B, H // heads_per_step),
        in_specs=in_specs,
        out_specs=pl.BlockSpec((None, T, gw), lambda b, g: (b, 0, g)),
        scratch_shapes=[pltpu.VMEM((n, gw), BF16) for n in scratch_rows],
        compiler_params=pltpu.CompilerParams(
            dimension_semantics=("parallel", "parallel"),
            vmem_limit_bytes=_vmem_limit(blocks, sum(_nbytes((n, gw), BF16) for n in scratch_rows), temps)),
        name="diffattn",
    )(*args)


def _merge_a_kernel(r_ref, d_ref, wr_ref, wd_ref, gr_ref, gd_ref, *rest):
    n_cast = len(rest) // 2
    cast_in, o_ref, cast_out = rest[:n_cast], rest[n_cast], rest[n_cast + 1:]
    for src, dst in zip(cast_in, cast_out):
        dst[...] = src[...].astype(dst.dtype)
    rb = jnp.dot(r_ref[...], wr_ref[...], preferred_element_type=F32)
    db = jnp.dot(d_ref[...], wd_ref[...], preferred_element_type=F32)
    gr = jax.nn.sigmoid(gr_ref[...].astype(F32))
    gd = jax.nn.sigmoid(gd_ref[...].astype(F32))
    o_ref[...] = (gr * rb + gd * db).astype(o_ref.dtype)


def _merge_a(ret_g, diff_n, w_ret_o, w_diff_o, gates, *, off_gr, off_gd, cast_params=(), tm=1024, tn=512):
    M, Kr = ret_g.shape
    Kd = diff_n.shape[1]
    D = w_ret_o.shape[1]
    tm = min(tm, M)
    n_j = D // tn
    blocks = (_nbytes((tm, Kr), BF16) + _nbytes((tm, Kd), BF16) + _nbytes((Kr, tn), BF16)
              + _nbytes((Kd, tn), BF16) + 3 * _nbytes((tm, tn), BF16))
    c_in, c_out, c_shapes, c_bytes = _slab_cast_specs(cast_params, (M // tm) * n_j, lambda i, j: i * n_j + j)
    outs = pl.pallas_call(
        _merge_a_kernel,
        out_shape=[jax.ShapeDtypeStruct((M, D), BF16)] + c_shapes,
        grid=(M // tm, n_j),
        in_specs=[pl.BlockSpec((tm, Kr), lambda i, j: (i, 0)),
                  pl.BlockSpec((tm, Kd), lambda i, j: (i, 0)),
                  pl.BlockSpec((Kr, tn), lambda i, j: (0, j)),
                  pl.BlockSpec((Kd, tn), lambda i, j: (0, j)),
                  pl.BlockSpec((tm, tn), lambda i, j: (i, off_gr // tn + j)),
                  pl.BlockSpec((tm, tn), lambda i, j: (i, off_gd // tn + j))] + c_in,
        out_specs=[pl.BlockSpec((tm, tn), lambda i, j: (i, j))] + c_out,
        compiler_params=pltpu.CompilerParams(
            dimension_semantics=("parallel", "arbitrary"),
            vmem_limit_bytes=_vmem_limit(blocks + c_bytes, temp_bytes=4 * _nbytes((tm, tn), F32))),
        name="merge_a",
    )(ret_g, diff_n, w_ret_o, w_diff_o, gates, gates, *cast_params)
    return outs if cast_params else outs[0]


MERGE_SUB_ROWS = 256


def _merge_b_kernel(m_ref, w_ref, x_ref, mod_ref, gpost_ref, gpre_ref, x1_ref, h2_ref):
    gate_post = mod_ref[2:3, :] * gpost_ref[...]
    a2 = gpre_ref[...] * (1.0 + mod_ref[4:5, :])
    shift2 = mod_ref[3:4, :]
    sub = MERGE_SUB_ROWS
    n_sub = x_ref.shape[0] // sub

    def out_proj(k):
        return jnp.dot(m_ref[k * sub:(k + 1) * sub, :], w_ref[...], preferred_element_type=F32)

    y_next = out_proj(0)
    for k in range(n_sub):
        y_all = y_next
        if k + 1 < n_sub:
            y_next = out_proj(k + 1)
        for c in range(sub // ROW_CHUNK):
            sl = slice(k * sub + c * ROW_CHUNK, k * sub + (c + 1) * ROW_CHUNK)
            y = y_all[c * ROW_CHUNK:(c + 1) * ROW_CHUNK, :]
            x1 = x_ref[sl, :] + y * lax.rsqrt(jnp.mean(y * y, axis=-1, keepdims=True) + EPS) * gate_post
            x1_ref[sl, :] = x1
            h2_ref[sl, :] = (x1 * lax.rsqrt(jnp.mean(x1 * x1, axis=-1, keepdims=True) + EPS) * a2
                             + shift2).astype(h2_ref.dtype)


def _merge_b(merged, w_out, x2, mod, g_post, g_pre, *, rows_per_group, tm=512):
    M, D = x2.shape
    tm = min(tm, M)
    blocks = (_nbytes((tm, D), BF16) * 2 + _nbytes((D, D), BF16) + 2 * _nbytes((tm, D), F32)
              + _nbytes((N_MOD + 2, D), F32))
    return pl.pallas_call(
        _merge_b_kernel,
        out_shape=[jax.ShapeDtypeStruct((M, D), F32), jax.ShapeDtypeStruct((M, D), BF16)],
        grid=(M // tm,),
        in_specs=[pl.BlockSpec((tm, D), lambda i: (i, 0)),
                  pl.BlockSpec((D, D), lambda i: (0, 0)),
                  pl.BlockSpec((tm, D), lambda i: (i, 0)),
                  pl.BlockSpec((None, N_MOD, D), _mod_map(mod.shape[0], rows_per_group, tm)),
                  pl.BlockSpec((1, D), lambda i: (0, 0)),
                  pl.BlockSpec((1, D), lambda i: (0, 0))],
        out_specs=[pl.BlockSpec((tm, D), lambda i: (i, 0)),
                   pl.BlockSpec((tm, D), lambda i: (i, 0))],
        compiler_params=pltpu.CompilerParams(
            dimension_semantics=("parallel",),
            vmem_limit_bytes=_vmem_limit(blocks, temp_bytes=2 * _nbytes((tm, D), F32))),
        name="merge_b",
    )(merged, w_out, x2, mod, g_post, g_pre)


def _mlp_kernel(h_ref, wu_ref, wd_ref, x_ref, mod_ref, g_ref, o_ref, acc_ref):
    j = pl.program_id(1)
    last = pl.num_programs(1) - 1

    def part(rows):
        u = jnp.maximum(jnp.dot(h_ref[rows, :], wu_ref[...], preferred_element_type=F32), 0.0)
        return jnp.dot((u * u).astype(BF16), wd_ref[...], preferred_element_type=F32)

    @pl.when(j == 0)
    def _():
        acc_ref[...] = part(slice(None))

    @pl.when(jnp.logical_and(j > 0, j < last))
    def _():
        acc_ref[...] += part(slice(None))

    @pl.when(j == last)
    def _():
        gate_post = mod_ref[5:6, :] * g_ref[...]
        sub = MERGE_SUB_ROWS
        n_sub = x_ref.shape[0] // sub
        p_next = part(slice(0, sub))
        for k in range(n_sub):
            p_cur = p_next
            if k + 1 < n_sub:
                p_next = part(slice((k + 1) * sub, (k + 2) * sub))
            for c in range(sub // ROW_CHUNK):
                sl = slice(k * sub + c * ROW_CHUNK, k * sub + (c + 1) * ROW_CHUNK)
                y = acc_ref[sl, :] + p_cur[c * ROW_CHUNK:(c + 1) * ROW_CHUNK, :]
                o_ref[sl, :] = (x_ref[sl, :]
                                + y * lax.rsqrt(jnp.mean(y * y, axis=-1, keepdims=True) + EPS) * gate_post)


def _mlp(h2, w_up, w_down, x1, mod, g_post, *, rows_per_group, tm=512, tf=1024):
    M, D = x1.shape
    Fh = w_up.shape[1]
    tm = min(tm, M)
    assert Fh // tf >= 2
    blocks = (_nbytes((tm, D), BF16) + 2 * _nbytes((D, tf), BF16) + 2 * _nbytes((tm, D), F32)
              + _nbytes((N_MOD + 1, D), F32))
    return pl.pallas_call(
        _mlp_kernel,
        out_shape=jax.ShapeDtypeStruct((M, D), F32),
        grid=(M // tm, Fh // tf),
        in_specs=[pl.BlockSpec((tm, D), lambda i, j: (i, 0)),
                  pl.BlockSpec((D, tf), lambda i, j: (0, j)),
                  pl.BlockSpec((tf, D), lambda i, j: (j, 0)),
                  pl.BlockSpec((tm, D), lambda i, j: (i, 0)),
                  pl.BlockSpec((None, N_MOD, D), _mod_map(mod.shape[0], rows_per_group, tm)),
                  pl.BlockSpec((1, D), lambda i, j: (0, 0))],
        out_specs=pl.BlockSpec((tm, D), lambda i, j: (i, 0)),
        scratch_shapes=[pltpu.VMEM((tm, D), F32)],
        compiler_params=pltpu.CompilerParams(
            dimension_semantics=("parallel", "arbitrary"),
            vmem_limit_bytes=_vmem_limit(blocks, _nbytes((tm, D), F32),
                                         2 * _nbytes((tm, tf), F32) + _nbytes((tm, D), F32))),
        name="mlp",
    )(h2, w_up, w_down, x1, mod, g_post)


def _rope_tables(T, rope_half):
    rows = T // GRID_W
    row = jnp.repeat(jnp.arange(rows, dtype=F32), GRID_W)
    col = jnp.tile(jnp.arange(GRID_W, dtype=F32), rows)
    inv = ROPE_BASE ** (-jnp.arange(0, rope_half, 2, dtype=F32) / rope_half)
    ar, ac = row[:, None] * inv, col[:, None] * inv
    z = jnp.zeros_like(ar)
    cos = jnp.concatenate([jnp.cos(ar), jnp.cos(ar), jnp.cos(ac), jnp.cos(ac)], axis=-1)
    sin_next = jnp.concatenate([-jnp.sin(ar), z, -jnp.sin(ac), z], axis=-1)
    sin_prev = jnp.concatenate([z, jnp.sin(ar), z, jnp.sin(ac)], axis=-1)
    return cos, sin_next, sin_prev


def kernel(x_prompt, x_sample, cache_k, cache_v, state_ret_fwd, state_ret_bwd, c, c_ctx, w_ada, b_ada, g_mix_pre, g_mix_post, g_mlp_pre, g_mlp_post, w_in, ret_gamma_logit_fwd, ret_gamma_logit_bwd, w_ret_o, lambda_q1, lambda_k1, lambda_q2, lambda_k2, g_diff_subln, w_diff_o, w_out, w_mlp_up, w_mlp_down):
    Bp, Tp, D = x_prompt.shape
    Bs, Ts, _ = x_sample.shape
    depth = w_in.shape[0]
    assert depth == 1
    l = 0
    H = N_RET_HEADS
    dk = D // H
    dv = 2 * dk
    dh = D // N_DIFF_HEADS // 2
    ret_qk_w, ret_v_w, diff_w = H * dk, H * dv, N_DIFF_HEADS * 2 * dh
    o_dq = 2 * ret_qk_w + 2 * ret_v_w
    o_dk, o_dv, o_gate = o_dq + diff_w, o_dq + 2 * diff_w, o_dq + 3 * diff_w
    in_w = o_gate + 2 * D
    assert w_in.shape[2] == in_w
    lam_init = 0.8 - 0.6 * math.exp(-0.3 * l)
    q_mult = float(dh) ** -0.5 * math.log2(math.e)

    w_in_l = w_in[l]
    g_mix_pre_l = g_mix_pre[l].reshape(1, D)
    g_mix_post_l = g_mix_post[l].reshape(1, D)
    g_mlp_pre_l = g_mlp_pre[l].reshape(1, D)
    g_mlp_post_l = g_mlp_post[l].reshape(1, D)
    g_subln_l = g_diff_subln[l].reshape(1, 2 * dh)
    lam_params = jnp.concatenate(
        [jnp.stack([lambda_q1[l], lambda_k1[l], lambda_q2[l], lambda_k2[l]]), jnp.zeros((4, dh), F32)], axis=0)
    lg = jnp.stack([ret_gamma_logit_fwd[l], ret_gamma_logit_bwd[l]], axis=1)
    lg = jnp.concatenate([lg, jnp.zeros((H, 6), F32)], axis=1)
    lg = jnp.broadcast_to(lg[:, :, None], (H, 8, V7X_LANES))

    n_cond = 16
    cond = jnp.concatenate([c, jnp.broadcast_to(c_ctx[None, :], (n_cond - Bs, D))], axis=0)
    mod = _adaln(cond, w_ada[l], b_ada[l].reshape(1, N_MOD * D)).reshape(n_cond, N_MOD, D)

    ret_offs = dict(off_q=0, off_k=ret_qk_w, off_v=2 * ret_qk_w, off_g=2 * ret_qk_w + ret_v_w)

    xs2 = x_sample.reshape(Bs * Ts, D)
    hs = _prenorm(xs2, mod, g_mix_pre_l, rows_per_group=Ts)
    sa = _proj(hs, w_in_l, (0, o_dq), out_dtype=BF16, tm=2048, name="proj_ret").reshape(Bs, Ts, -1)
    sqk, w_ret_o_b, w_diff_o_b, w_out_b = _proj(
        hs, w_in_l, (o_dq, o_dv), out_dtype=BF16, n_q_cols=diff_w, q_mult=q_mult,
        rope_tables=_rope_tables(Ts, dh // 2), cast_params=(w_ret_o[l], w_diff_o[l], w_out[l]), name="proj_qk_rope")
    sqk = sqk.reshape(Bs, Ts, 2 * diff_w)
    svg = _proj(hs, w_in_l, (o_dv, in_w), out_dtype=BF16, tm=2048, name="proj_vg")
    (ret_g,) = _retention(sa, lg, (state_ret_fwd, state_ret_bwd), emit_state=False, dk=dk, dv=dv, **ret_offs)
    caches = (cache_k[:, l].reshape(Bs, -1, diff_w), cache_v[:, l].reshape(Bs, -1, diff_w))
    diff_n = _diffattn((sqk, 0), (sqk, diff_w), (svg.reshape(Bs, Ts, -1), 0),
                       lam_params, g_subln_l, caches, lam_init=lam_init, dh=dh, heads_per_step=1)
    merged, w_up_b, w_down_b = _merge_a(
        ret_g.reshape(Bs * Ts, ret_v_w), diff_n.reshape(Bs * Ts, diff_w), w_ret_o_b, w_diff_o_b, svg,
        off_gr=diff_w, off_gd=diff_w + D, cast_params=(w_mlp_up[l], w_mlp_down[l]))

    def tail(merged, x2, B, T, rows_per_group):
        x1, h2 = _merge_b(merged, w_out_b, x2, mod, g_mix_post_l, g_mlp_pre_l, rows_per_group=rows_per_group)
        y = _mlp(h2, w_up_b, w_down_b, x1, mod, g_mlp_post_l, rows_per_group=rows_per_group)
        return y.reshape(B, T, D)

    y_sample = tail(merged, xs2, Bs, Ts, Ts)

    xp2 = x_prompt.reshape(Bp * Tp, D)
    hp = _prenorm(xp2, mod, g_mix_pre_l, rows_per_group=None)
    pa = _proj(hp, w_in_l, (0, o_dq), out_dtype=BF16, tm=2048, name="proj_ret").reshape(Bp, Tp, -1)
    pq = _proj(hp, w_in_l, (o_dq, o_dk), out_dtype=BF16, n_q_cols=diff_w, q_mult=q_mult,
               name="proj_q").reshape(Bp, Tp, diff_w)
    pk = _proj(hp, w_in_l, (o_dk, o_dv), out_dtype=F32, name="proj_k")
    pv = _proj(hp, w_in_l, (o_dv, o_gate), out_dtype=F32, name="proj_v")
    pg = _proj(hp, w_in_l, (o_gate, in_w), out_dtype=BF16, tm=2048, name="proj_gates")
    ret_g, new_state_fwd, new_state_bwd = _retention(pa, lg, None, emit_state=True, dk=dk, dv=dv,
                                                     batch_block=4, **ret_offs)
    diff_n = _diffattn((pq, 0), (pk.reshape(Bp, Tp, diff_w), 0), (pv.reshape(Bp, Tp, diff_w), 0),
                       lam_params, g_subln_l, None, lam_init=lam_init, dh=dh, heads_per_step=N_DIFF_HEADS)
    merged = _merge_a(ret_g.reshape(Bp * Tp, ret_v_w), diff_n.reshape(Bp * Tp, diff_w), w_ret_o_b, w_diff_o_b, pg,
                      off_gr=0, off_gd=D)
    y_prompt = tail(merged, xp2, Bp, Tp, None)
    new_cache_k = pk.reshape(Bp, 1, Tp, N_DIFF_HEADS, 2, dh)
    new_cache_v = pv.reshape(Bp, 1, Tp, N_DIFF_HEADS, 2 * dh)

    return (y_prompt, y_sample, new_cache_k, new_cache_v, new_state_fwd, new_state_bwd)
```

```python
import functools
import math

import jax
import jax.numpy as jnp
from jax import lax
from jax.experimental import pallas as pl
from jax.experimental.pallas import tpu as pltpu

F32 = jnp.float32
BF16 = jnp.bfloat16

N_RET_HEADS = 8
N_DIFF_HEADS = 8
N_MOD = 6
GRID_W = 64
ROPE_BASE = 10000.0
EPS = 1e-6
SUBLN_EPS = 1e-5

V7X_LANES = 128
V7X_VMEM_BYTES = 64 * 1024 * 1024
VMEM_RESERVE_BYTES = 8 * 1024 * 1024


def _vmem_limit(block_bytes, scratch_bytes=0, temp_bytes=0):
    want = 2 * block_bytes + scratch_bytes + temp_bytes + VMEM_RESERVE_BYTES
    return int(min(want, V7X_VMEM_BYTES - VMEM_RESERVE_BYTES // 2))


def _nbytes(shape, dtype):
    return math.prod(shape) * jnp.dtype(dtype).itemsize


def _mod_map(n_groups, rows_per_group, tm):
    if rows_per_group is None:
        return lambda i, *_: (n_groups - 1, 0, 0)
    return lambda i, *_: ((i * tm) // rows_per_group, 0, 0)


def _adaln_kernel(c_ref, w_ref, b_ref, o_ref):
    c = c_ref[...]
    s = (c * jax.nn.sigmoid(c)).astype(BF16)
    o_ref[...] = jnp.dot(s, w_ref[...].astype(BF16), preferred_element_type=F32) + b_ref[...]


def _adaln(cond, w_ada, b_ada, tn=512):
    R, D = cond.shape
    N = w_ada.shape[1]
    blocks = _nbytes((R, D), F32) + _nbytes((D, tn), F32) + _nbytes((1, tn), F32) + _nbytes((R, tn), F32)
    return pl.pallas_call(
        _adaln_kernel,
        out_shape=jax.ShapeDtypeStruct((R, N), F32),
        grid=(N // tn,),
        in_specs=[pl.BlockSpec((R, D), lambda j: (0, 0)),
                  pl.BlockSpec((D, tn), lambda j: (0, j)),
                  pl.BlockSpec((1, tn), lambda j: (0, j))],
        out_specs=pl.BlockSpec((R, tn), lambda j: (0, j)),
        compiler_params=pltpu.CompilerParams(
            dimension_semantics=("arbitrary",),
            vmem_limit_bytes=_vmem_limit(blocks, temp_bytes=_nbytes((D, tn), BF16))),
        name="adaln",
    )(cond, w_ada, b_ada)


ROW_CHUNK = 64
NORM_UNROLL = 4


def _norm_modulate(x_ref, gain, shift, scale, h_ref):
    a = gain * (1.0 + scale)
    n = x_ref.shape[0] // ROW_CHUNK

    def body(r, carry):
        sl = pl.ds(pl.multiple_of(r * ROW_CHUNK, ROW_CHUNK), ROW_CHUNK)
        x = x_ref[sl, :]
        ms = jnp.mean(x * x, axis=-1, keepdims=True)
        h_ref[sl, :] = (x * lax.rsqrt(ms + EPS) * a + shift).astype(h_ref.dtype)
        return carry

    lax.fori_loop(0, n, body, 0, unroll=NORM_UNROLL)


def _prenorm_kernel(x_ref, mod_ref, g_ref, h_ref):
    _norm_modulate(x_ref, g_ref[...], mod_ref[0:1, :], mod_ref[1:2, :], h_ref)


def _prenorm(x2, mod, gain, *, rows_per_group, tm=1024):
    M, D = x2.shape
    blocks = _nbytes((tm, D), F32) + _nbytes((tm, D), BF16) + _nbytes((N_MOD + 1, D), F32)
    return pl.pallas_call(
        _prenorm_kernel,
        out_shape=jax.ShapeDtypeStruct((M, D), BF16),
        grid=(M // tm,),
        in_specs=[pl.BlockSpec((tm, D), lambda i: (i, 0)),
                  pl.BlockSpec((None, N_MOD, D), _mod_map(mod.shape[0], rows_per_group, tm)),
                  pl.BlockSpec((1, D), lambda i: (0, 0))],
        out_specs=pl.BlockSpec((tm, D), lambda i: (i, 0)),
        compiler_params=pltpu.CompilerParams(
            dimension_semantics=("parallel",),
            vmem_limit_bytes=_vmem_limit(blocks, temp_bytes=4 * _nbytes((ROW_CHUNK, D), F32))),
        name="prenorm",
    )(x2, mod, gain)


CAST_ROWS = 256
ROPE_GROUP = V7X_LANES // 4


def _pair_rotary_lanes(x):
    n = x.shape[-1]
    group = (lax.broadcasted_iota(jnp.int32, x.shape, x.ndim - 1) % V7X_LANES) // ROPE_GROUP
    from_next = pltpu.roll(x, n - ROPE_GROUP, x.ndim - 1)
    from_prev = pltpu.roll(x, ROPE_GROUP, x.ndim - 1)
    return jnp.where(group == 1, from_next, jnp.where(group == 2, from_prev, x))


def _slab_cast_specs(params, n_steps, step_index):
    in_specs, out_specs, out_shapes, nbytes = [], [], [], 0
    for p in params:
        rows, width = p.shape[0] // n_steps, p.shape[1]
        assert rows * n_steps == p.shape[0] and rows % 16 == 0
        spec = pl.BlockSpec((rows, width), lambda *ids: (step_index(*ids), 0))
        in_specs.append(spec)
        out_specs.append(spec)
        out_shapes.append(jax.ShapeDtypeStruct(p.shape, BF16))
        nbytes += _nbytes((rows, width), F32) + _nbytes((rows, width), BF16)
    return in_specs, out_specs, out_shapes, nbytes


def _proj_kernel(h_ref, w_ref, *rest, n_q_blocks, q_mult, rope, n_cast):
    rest = list(rest)
    if rope:
        cos_ref, sin_ref = rest[:2]
        rest = rest[2:]
    cast_in, o_ref, cast_out, wb_ref = rest[:n_cast], rest[n_cast], rest[n_cast + 1:2 * n_cast + 1], rest[-1]
    for src, dst in zip(cast_in, cast_out):
        dst[...] = src[...].astype(dst.dtype)

    @pl.when(pl.program_id(1) == 0)
    def _():
        def body(r, carry):
            sl = pl.ds(pl.multiple_of(r * CAST_ROWS, CAST_ROWS), CAST_ROWS)
            wf = w_ref[sl, :]
            wb_ref[sl, :] = (_pair_rotary_lanes(wf) if rope else wf).astype(wb_ref.dtype)
            return carry

        lax.fori_loop(0, w_ref.shape[0] // CAST_ROWS, body, 0)

    mult = None
    if n_q_blocks:
        mult = jnp.where(pl.program_id(0) < n_q_blocks, q_mult, 1.0).astype(F32)
    acc = jnp.dot(h_ref[...], wb_ref[...], preferred_element_type=F32)
    if rope:
        cos, sin = cos_ref[...] * mult, sin_ref[...] * mult
        for c in range(acc.shape[1] // V7X_LANES):
            cs = slice(c * V7X_LANES, (c + 1) * V7X_LANES)
            xc = acc[:, cs]
            o_ref[:, cs] = (xc * cos + pltpu.roll(xc, V7X_LANES // 2, 1) * sin).astype(o_ref.dtype)
    elif mult is not None:
        o_ref[...] = (acc * mult).astype(o_ref.dtype)
    else:
        o_ref[...] = acc.astype(o_ref.dtype)


def _proj(h, w, cols, *, out_dtype, n_q_cols=0, q_mult=1.0, rope_tables=None, cast_params=(),
          tm=1024, tn=1024, name="proj"):
    M, D = h.shape
    N = cols[1] - cols[0]
    tm = min(tm, M)
    tn = min(tn, N)
    j0 = cols[0] // tn
    in_specs = [pl.BlockSpec((tm, D), lambda j, i: (i, 0)),
                pl.BlockSpec((D, tn), lambda j, i: (0, j0 + j))]
    args = [h, w]
    blocks = _nbytes((tm, D), BF16) + _nbytes((D, tn), w.dtype) + _nbytes((tm, tn), out_dtype)
    if rope_tables is not None:
        nt = rope_tables[0].shape[0] // tm
        for t in rope_tables:
            in_specs.append(pl.BlockSpec((tm, V7X_LANES), lambda j, i: (i % nt, 0)))
            args.append(t)
        blocks += len(rope_tables) * _nbytes((tm, V7X_LANES), F32)
    n_i = M // tm
    c_in, c_out, c_shapes, c_bytes = _slab_cast_specs(cast_params, (N // tn) * n_i, lambda j, i: j * n_i + i)
    outs = pl.pallas_call(
        functools.partial(_proj_kernel, n_q_blocks=n_q_cols // tn, q_mult=q_mult, rope=rope_tables is not None,
                          n_cast=len(cast_params)),
        out_shape=[jax.ShapeDtypeStruct((M, N), out_dtype)] + c_shapes,
        grid=(N // tn, n_i),
        in_specs=in_specs + c_in,
        out_specs=[pl.BlockSpec((tm, tn), lambda j, i: (i, j))] + c_out,
        scratch_shapes=[pltpu.VMEM((D, tn), BF16)],
        compiler_params=pltpu.CompilerParams(
            dimension_semantics=("parallel", "arbitrary"),
            vmem_limit_bytes=_vmem_limit(blocks + c_bytes, _nbytes((D, tn), BF16), _nbytes((tm, tn), F32))),
        name=name,
    )(*args, *cast_params)
    return outs if cast_params else outs[0]


RET_CHUNK = 256


def _log_sigmoid(x):
    return jnp.minimum(x, 0.0) - jnp.log1p(jnp.exp(-jnp.abs(x)))


def _ret_kernel(lg_ref, q_ref, k_ref, v_ref, g_ref, *rest, has_state, emit_state, k_scale):
    rest = list(rest)
    s0f_ref = s0b_ref = sf_out = sb_out = None
    if has_state:
        s0f_ref, s0b_ref = rest[:2]
        rest = rest[2:]
    o_ref = rest[0]
    rest = rest[1:]
    if emit_state:
        sf_out, sb_out = rest[:2]
        rest = rest[2:]
    (st_scr,) = rest

    bb, T, dk = q_ref.shape
    C = min(RET_CHUNK, T)
    n = T // C

    lsig = _log_sigmoid(lg_ref[...])
    lgf = lsig[0:1, 0:1]
    lgb = lsig[1:2, 0:1]
    ri = lax.broadcasted_iota(jnp.int32, (C, C), 0)
    ci = lax.broadcasted_iota(jnp.int32, (C, C), 1)
    rel = (ri - ci).astype(F32)
    decay = jnp.where(rel > 0, jnp.exp(rel * lgf), jnp.where(rel < 0, jnp.exp(-rel * lgb), 2.0)) * k_scale
    li = lax.broadcasted_iota(jnp.int32, (C, dk), 0).astype(F32)
    xi_f = jnp.exp((li + 1.0) * lgf)
    xi_b = jnp.exp((C - li) * lgb)
    zeta_f = jnp.exp((C - 1.0 - li) * lgf) * k_scale
    zeta_b = jnp.exp(li * lgb) * k_scale
    gc_f = jnp.exp(C * lgf)
    gc_b = jnp.exp(C * lgb)

    def rows(c):
        return slice(c * C, (c + 1) * C)

    def fwd_state_used(c):
        return has_state or c > 0

    def bwd_state_used(c):
        return has_state or c < n - 1

    def kv_pair(item):
        b, t = item
        out = []
        for c, zeta in ((t, zeta_f), (n - 1 - t, zeta_b)):
            ks = (k_ref[b, rows(c), :].astype(F32) * zeta).astype(BF16)
            out.append(lax.dot_general(ks, v_ref[b, rows(c), :], (((0,), (0,)), ((), ())),
                                       preferred_element_type=F32))
        return out

    n_upd = n if emit_state else n - 1
    items = [(b, t) for b in range(bb) for t in range(n_upd)]
    kv_next = kv_pair(items[0]) if items else None
    pos = 0
    for b in range(bb):
        sf = s0f_ref[b] if has_state else None
        sb = s0b_ref[b] if has_state else None
        for t in range(n):
            cf, cb = t, n - 1 - t
            if fwd_state_used(cf):
                st_scr[b, cf, 0:dk, :] = sf.astype(BF16)
            if bwd_state_used(cb):
                st_scr[b, cb, dk:2 * dk, :] = sb.astype(BF16)
            if t < n_upd:
                kvf, kvb = kv_next
                pos += 1
                if pos < len(items):
                    kv_next = kv_pair(items[pos])
                sf = kvf if sf is None else gc_f * sf + kvf
                sb = kvb if sb is None else gc_b * sb + kvb
        if emit_state:
            sf_out[b] = sf
            sb_out[b] = sb

    def chunk_out(item):
        b, c = item
        qc = q_ref[b, rows(c), :]
        s = lax.dot_general(qc, k_ref[b, rows(c), :], (((1,), (1,)), ((), ())),
                            preferred_element_type=F32) * decay
        o = jnp.dot(s.astype(BF16), v_ref[b, rows(c), :], preferred_element_type=F32)
        qf = qc.astype(F32)
        parts = []
        if fwd_state_used(c):
            parts.append(((qf * xi_f).astype(BF16), st_scr[b, c, 0:dk, :]))
        if bwd_state_used(c):
            parts.append(((qf * xi_b).astype(BF16), st_scr[b, c, dk:2 * dk, :]))
        if len(parts) == 2:
            o = o + jnp.dot(jnp.concatenate([parts[0][0], parts[1][0]], axis=1), st_scr[b, c],
                            preferred_element_type=F32)
        elif parts:
            o = o + jnp.dot(parts[0][0], parts[0][1], preferred_element_type=F32)
        return o

    items = [(b, c) for b in range(bb) for c in range(n)]
    o_next = chunk_out(items[0])
    for pos, (b, c) in enumerate(items):
        o = o_next
        if pos + 1 < len(items):
            o_next = chunk_out(items[pos + 1])
        on = o * lax.rsqrt(jnp.mean(o * o, axis=-1, keepdims=True) + EPS)
        hg = g_ref[b, rows(c), :] * 0.5
        o_ref[b, rows(c), :] = (hg + hg * jnp.tanh(hg)) * on.astype(o_ref.dtype)


def _retention(pa3, lg, states, *, emit_state, dk, dv, off_q, off_k, off_v, off_g, batch_block=1):
    B, T, _ = pa3.shape
    H = N_RET_HEADS
    bb = batch_block
    has_state = states is not None
    C = min(RET_CHUNK, T)
    n = T // C
    in_specs = [pl.BlockSpec((None, 8, V7X_LANES), lambda b, h: (h, 0, 0)),
                pl.BlockSpec((bb, T, dk), lambda b, h: (b, 0, off_q // dk + h)),
                pl.BlockSpec((bb, T, dk), lambda b, h: (b, 0, off_k // dk + h)),
                pl.BlockSpec((bb, T, dv), lambda b, h: (b, 0, off_v // dv + h)),
                pl.BlockSpec((bb, T, dv), lambda b, h: (b, 0, off_g // dv + h))]
    args = [lg, pa3, pa3, pa3, pa3]
    blocks = bb * (2 * _nbytes((T, dk), BF16) + 3 * _nbytes((T, dv), BF16)) + _nbytes((8, V7X_LANES), F32)
    state_spec = pl.BlockSpec((bb, None, None, dk, dv), lambda b, h: (b, 0, h, 0, 0))
    if has_state:
        in_specs += [state_spec, state_spec]
        args += list(states)
        blocks += 2 * bb * _nbytes((dk, dv), F32)
    out_shape = [jax.ShapeDtypeStruct((B, T, H * dv), BF16)]
    out_specs = [pl.BlockSpec((bb, T, dv), lambda b, h: (b, 0, h))]
    if emit_state:
        out_shape += [jax.ShapeDtypeStruct((B, 1, H, dk, dv), F32)] * 2
        out_specs += [state_spec, state_spec]
        blocks += 2 * bb * _nbytes((dk, dv), F32)
    scratch_bytes = _nbytes((bb, n, 2 * dk, dv), BF16)
    temps = 8 * _nbytes((C, dv), F32) + 8 * _nbytes((dk, dv), F32) + 4 * _nbytes((C, C), F32)
    return pl.pallas_call(
        functools.partial(_ret_kernel, has_state=has_state, emit_state=emit_state, k_scale=float(dk) ** -0.5),
        out_shape=out_shape,
        grid=(B // bb, H),
        in_specs=in_specs,
        out_specs=out_specs,
        scratch_shapes=[pltpu.VMEM((bb, n, 2 * dk, dv), BF16)],
        compiler_params=pltpu.CompilerParams(
            dimension_semantics=("parallel", "parallel"),
            vmem_limit_bytes=_vmem_limit(blocks, scratch_bytes, temps)),
        name="retention",
    )(*args)


DIFF_SUB_ROWS = 256


def _diff_kernel(lam_ref, gs_ref, q_ref, k_ref, v_ref, *rest, has_cache, lam_init, dh, sub):
    if has_cache:
        ck_ref, cv_ref, o_ref, kall, vall = rest
    else:
        o_ref, kall, vall = rest
    T = k_ref.shape[0]
    w = 2 * dh
    heads = q_ref.shape[1] // w

    kall[0:T, :] = k_ref[...].astype(kall.dtype)
    vall[0:T, :] = v_ref[...].astype(vall.dtype)
    if has_cache:
        kall[T:, :] = _pair_rotary_lanes(ck_ref[...]).astype(kall.dtype)
        vall[T:, :] = cv_ref[...].astype(vall.dtype)

    lp = lam_ref[...]
    lam = (jnp.exp(jnp.sum(lp[0:1, :] * lp[1:2, :], axis=-1, keepdims=True))
           - jnp.exp(jnp.sum(lp[2:3, :] * lp[3:4, :], axis=-1, keepdims=True)) + lam_init)
    gain = gs_ref[...] * (1.0 - lam_init)
    items = [(hh, r) for hh in range(heads) for r in range(T // sub)]

    def scores(item):
        hh, r = item
        out = []
        for m in range(2):
            cs = slice(hh * w + m * dh, hh * w + (m + 1) * dh)
            out.append(lax.dot_general(q_ref[r * sub:(r + 1) * sub, cs], kall[:, cs],
                                       (((1,), (1,)), ((), ())), preferred_element_type=F32))
        return out

    s_next = scores(items[0])
    for t, (hh, r) in enumerate(items):
        s_cur = s_next
        if t + 1 < len(items):
            s_next = scores(items[t + 1])
        probs = []
        for s in s_cur:
            e = jnp.exp2(s - jnp.max(s, axis=-1, keepdims=True))
            probs.append((e, jnp.sum(e, axis=-1, keepdims=True)))
        (e1, l1), (e2, l2) = probs
        r1 = 1.0 / l1
        a = (e1 - e2 * (lam * l1 / l2)).astype(BF16)
        o = jnp.dot(a, vall[:, hh * w:(hh + 1) * w], preferred_element_type=F32) * r1
        on = o * lax.rsqrt(jnp.mean(o * o, axis=-1, keepdims=True) + SUBLN_EPS)
        o_ref[r * sub:(r + 1) * sub, hh * w:(hh + 1) * w] = (on * gain).astype(o_ref.dtype)


def _diffattn(q_src, k_src, v_src, lam_params, g_subln, caches, *, lam_init, dh, heads_per_step):
    (qa, off_q), (ka, off_k), (va, off_v) = q_src, k_src, v_src
    B, T, _ = qa.shape
    H = N_DIFF_HEADS
    w = 2 * dh
    gw = heads_per_step * w
    sub = min(DIFF_SUB_ROWS, T)
    has_cache = caches is not None
    Tk = T + (caches[0].shape[1] if has_cache else 0)
    in_specs = [pl.BlockSpec((8, dh), lambda b, g: (0, 0)),
                pl.BlockSpec((1, w), lambda b, g: (0, 0)),
                pl.BlockSpec((None, T, gw), lambda b, g: (b, 0, off_q // gw + g)),
                pl.BlockSpec((None, T, gw), lambda b, g: (b, 0, off_k // gw + g)),
                pl.BlockSpec((None, T, gw), lambda b, g: (b, 0, off_v // gw + g))]
    args = [lam_params, g_subln, qa, ka, va]
    blocks = 2 * _nbytes((T, gw), BF16) + _nbytes((T, gw), ka.dtype) + _nbytes((T, gw), va.dtype)
    if has_cache:
        P = caches[0].shape[1]
        cspec = pl.BlockSpec((None, P, gw), lambda b, g: (b, 0, g))
        in_specs += [cspec, cspec]
        args += list(caches)
        blocks += 2 * _nbytes((P, gw), F32)
    temps = 10 * _nbytes((sub, Tk), F32)
    scratch_rows = [Tk, Tk]
    return pl.pallas_call(
        functools.partial(_diff_kernel, has_cache=has_cache, lam_init=lam_init, dh=dh, sub=sub),
        out_shape=jax.ShapeDtypeStruct((B, T, H * w), BF16),
        grid=(B, H // heads_per_step),
        in_specs=in_specs,
        out_specs=pl.BlockSpec((None, T, gw), lambda b, g: (b, 0, g)),
        scratch_shapes=[pltpu.VMEM((n, gw), BF16) for n in scratch_rows],
        compiler_params=pltpu.CompilerParams(
            dimension_semantics=("parallel", "parallel"),
            vmem_limit_bytes=_vmem_limit(blocks, sum(_nbytes((n, gw), BF16) for n in scratch_rows), temps)),
        name="diffattn",
    )(*args)


def _merge_a_kernel(r_ref, d_ref, wr_ref, wd_ref, gr_ref, gd_ref, *rest):
    n_cast = len(rest) // 2
    cast_in, o_ref, cast_out = rest[:n_cast], rest[n_cast], rest[n_cast + 1:]
    for src, dst in zip(cast_in, cast_out):
        dst[...] = src[...].astype(dst.dtype)
    rb = jnp.dot(r_ref[...], wr_ref[...], preferred_element_type=F32)
    db = jnp.dot(d_ref[...], wd_ref[...], preferred_element_type=F32)
    gr = jax.nn.sigmoid(gr_ref[...].astype(F32))
    gd = jax.nn.sigmoid(gd_ref[...].astype(F32))
    o_ref[...] = (gr * rb + gd * db).astype(o_ref.dtype)


def _merge_a(ret_g, diff_n, w_ret_o, w_diff_o, gates, *, off_gr, off_gd, cast_params=(), tm=1024, tn=512):
    M, Kr = ret_g.shape
    Kd = diff_n.shape[1]
    D = w_ret_o.shape[1]
    tm = min(tm, M)
    n_j = D // tn
    blocks = (_nbytes((tm, Kr), BF16) + _nbytes((tm, Kd), BF16) + _nbytes((Kr, tn), BF16)
              + _nbytes((Kd, tn), BF16) + 3 * _nbytes((tm, tn), BF16))
    c_in, c_out, c_shapes, c_bytes = _slab_cast_specs(cast_params, (M // tm) * n_j, lambda i, j: i * n_j + j)
    outs = pl.pallas_call(
        _merge_a_kernel,
        out_shape=[jax.ShapeDtypeStruct((M, D), BF16)] + c_shapes,
        grid=(M // tm, n_j),
        in_specs=[pl.BlockSpec((tm, Kr), lambda i, j: (i, 0)),
                  pl.BlockSpec((tm, Kd), lambda i, j: (i, 0)),
                  pl.BlockSpec((Kr, tn), lambda i, j: (0, j)),
                  pl.BlockSpec((Kd, tn), lambda i, j: (0, j)),
                  pl.BlockSpec((tm, tn), lambda i, j: (i, off_gr // tn + j)),
                  pl.BlockSpec((tm, tn), lambda i, j: (i, off_gd // tn + j))] + c_in,
        out_specs=[pl.BlockSpec((tm, tn), lambda i, j: (i, j))] + c_out,
        compiler_params=pltpu.CompilerParams(
            dimension_semantics=("parallel", "arbitrary"),
            vmem_limit_bytes=_vmem_limit(blocks + c_bytes, temp_bytes=4 * _nbytes((tm, tn), F32))),
        name="merge_a",
    )(ret_g, diff_n, w_ret_o, w_diff_o, gates, gates, *cast_params)
    return outs if cast_params else outs[0]


MERGE_SUB_ROWS = 256


def _merge_b_kernel(m_ref, w_ref, x_ref, mod_ref, gpost_ref, gpre_ref, x1_ref, h2_ref):
    gate_post = mod_ref[2:3, :] * gpost_ref[...]
    a2 = gpre_ref[...] * (1.0 + mod_ref[4:5, :])
    shift2 = mod_ref[3:4, :]
    sub = MERGE_SUB_ROWS
    n_sub = x_ref.shape[0] // sub

    def out_proj(k):
        return jnp.dot(m_ref[k * sub:(k + 1) * sub, :], w_ref[...], preferred_element_type=F32)

    y_next = out_proj(0)
    for k in range(n_sub):
        y_all = y_next
        if k + 1 < n_sub:
            y_next = out_proj(k + 1)
        for c in range(sub // ROW_CHUNK):
            sl = slice(k * sub + c * ROW_CHUNK, k * sub + (c + 1) * ROW_CHUNK)
            y = y_all[c * ROW_CHUNK:(c + 1) * ROW_CHUNK, :]
            x1 = x_ref[sl, :] + y * lax.rsqrt(jnp.mean(y * y, axis=-1, keepdims=True) + EPS) * gate_post
            x1_ref[sl, :] = x1
            h2_ref[sl, :] = (x1 * lax.rsqrt(jnp.mean(x1 * x1, axis=-1, keepdims=True) + EPS) * a2
                             + shift2).astype(h2_ref.dtype)


def _merge_b(merged, w_out, x2, mod, g_post, g_pre, *, rows_per_group, tm=512):
    M, D = x2.shape
    tm = min(tm, M)
    blocks = (_nbytes((tm, D), BF16) * 2 + _nbytes((D, D), BF16) + 2 * _nbytes((tm, D), F32)
              + _nbytes((N_MOD + 2, D), F32))
    return pl.pallas_call(
        _merge_b_kernel,
        out_shape=[jax.ShapeDtypeStruct((M, D), F32), jax.ShapeDtypeStruct((M, D), BF16)],
        grid=(M // tm,),
        in_specs=[pl.BlockSpec((tm, D), lambda i: (i, 0)),
                  pl.BlockSpec((D, D), lambda i: (0, 0)),
                  pl.BlockSpec((tm, D), lambda i: (i, 0)),
                  pl.BlockSpec((None, N_MOD, D), _mod_map(mod.shape[0], rows_per_group, tm)),
                  pl.BlockSpec((1, D), lambda i: (0, 0)),
                  pl.BlockSpec((1, D), lambda i: (0, 0))],
        out_specs=[pl.BlockSpec((tm, D), lambda i: (i, 0)),
                   pl.BlockSpec((tm, D), lambda i: (i, 0))],
        compiler_params=pltpu.CompilerParams(
            dimension_semantics=("parallel",),
            vmem_limit_bytes=_vmem_limit(blocks, temp_bytes=2 * _nbytes((tm, D), F32))),
        name="merge_b",
    )(merged, w_out, x2, mod, g_post, g_pre)


def _mlp_kernel(h_ref, wu_ref, wd_ref, x_ref, mod_ref, g_ref, o_ref):
    j = pl.program_id(1)
    last = pl.num_programs(1) - 1

    def part(rows):
        u = jnp.maximum(jnp.dot(h_ref[rows, :], wu_ref[...], preferred_element_type=F32), 0.0)
        return jnp.dot((u * u).astype(BF16), wd_ref[...], preferred_element_type=F32)

    @pl.when(j == 0)
    def _():
        o_ref[...] = part(slice(None))

    @pl.when(jnp.logical_and(j > 0, j < last))
    def _():
        o_ref[...] += part(slice(None))

    @pl.when(j == last)
    def _():
        gate_post = mod_ref[5:6, :] * g_ref[...]
        sub = MERGE_SUB_ROWS
        n_sub = x_ref.shape[0] // sub
        p_next = part(slice(0, sub))
        for k in range(n_sub):
            p_cur = p_next
            if k + 1 < n_sub:
                p_next = part(slice((k + 1) * sub, (k + 2) * sub))
            for c in range(sub // ROW_CHUNK):
                sl = slice(k * sub + c * ROW_CHUNK, k * sub + (c + 1) * ROW_CHUNK)
                y = o_ref[sl, :] + p_cur[c * ROW_CHUNK:(c + 1) * ROW_CHUNK, :]
                o_ref[sl, :] = (x_ref[sl, :]
                                + y * lax.rsqrt(jnp.mean(y * y, axis=-1, keepdims=True) + EPS) * gate_post)


def _mlp(h2, w_up, w_down, x1, mod, g_post, *, rows_per_group, tm=1024, tf=512):
    M, D = x1.shape
    Fh = w_up.shape[1]
    tm = min(tm, M)
    assert Fh // tf >= 2
    blocks = (_nbytes((tm, D), BF16) + 2 * _nbytes((D, tf), BF16) + 2 * _nbytes((tm, D), F32)
              + _nbytes((N_MOD + 1, D), F32))
    return pl.pallas_call(
        _mlp_kernel,
        out_shape=jax.ShapeDtypeStruct((M, D), F32),
        grid=(M // tm, Fh // tf),
        in_specs=[pl.BlockSpec((tm, D), lambda i, j: (i, 0)),
                  pl.BlockSpec((D, tf), lambda i, j: (0, j)),
                  pl.BlockSpec((tf, D), lambda i, j: (j, 0)),
                  pl.BlockSpec((tm, D), lambda i, j: (i, 0)),
                  pl.BlockSpec((None, N_MOD, D), _mod_map(mod.shape[0], rows_per_group, tm)),
                  pl.BlockSpec((1, D), lambda i, j: (0, 0))],
        out_specs=pl.BlockSpec((tm, D), lambda i, j: (i, 0)),
        compiler_params=pltpu.CompilerParams(
            dimension_semantics=("parallel", "arbitrary"),
            vmem_limit_bytes=_vmem_limit(blocks, temp_bytes=2 * _nbytes((tm, tf), F32))),
        name="mlp",
    )(h2, w_up, w_down, x1, mod, g_post)


def _rope_tables(T, rope_half):
    rows = T // GRID_W
    row = jnp.repeat(jnp.arange(rows, dtype=F32), GRID_W)
    col = jnp.tile(jnp.arange(GRID_W, dtype=F32), rows)
    inv = ROPE_BASE ** (-jnp.arange(0, rope_half, 2, dtype=F32) / rope_half)
    ar, ac = row[:, None] * inv, col[:, None] * inv
    cos = jnp.concatenate([jnp.cos(ar), jnp.cos(ac), jnp.cos(ar), jnp.cos(ac)], axis=-1)
    sin = jnp.concatenate([-jnp.sin(ar), -jnp.sin(ac), jnp.sin(ar), jnp.sin(ac)], axis=-1)
    return cos, sin


def kernel(x_prompt, x_sample, cache_k, cache_v, state_ret_fwd, state_ret_bwd, c, c_ctx, w_ada, b_ada, g_mix_pre, g_mix_post, g_mlp_pre, g_mlp_post, w_in, ret_gamma_logit_fwd, ret_gamma_logit_bwd, w_ret_o, lambda_q1, lambda_k1, lambda_q2, lambda_k2, g_diff_subln, w_diff_o, w_out, w_mlp_up, w_mlp_down):
    Bp, Tp, D = x_prompt.shape
    Bs, Ts, _ = x_sample.shape
    depth = w_in.shape[0]
    assert depth == 1
    l = 0
    H = N_RET_HEADS
    dk = D // H
    dv = 2 * dk
    dh = D // N_DIFF_HEADS // 2
    ret_qk_w, ret_v_w, diff_w = H * dk, H * dv, N_DIFF_HEADS * 2 * dh
    o_dq = 2 * ret_qk_w + 2 * ret_v_w
    o_dk, o_dv, o_gate = o_dq + diff_w, o_dq + 2 * diff_w, o_dq + 3 * diff_w
    in_w = o_gate + 2 * D
    assert w_in.shape[2] == in_w
    lam_init = 0.8 - 0.6 * math.exp(-0.3 * l)
    q_mult = float(dh) ** -0.5 * math.log2(math.e)

    w_in_l = w_in[l]
    g_mix_pre_l = g_mix_pre[l].reshape(1, D)
    g_mix_post_l = g_mix_post[l].reshape(1, D)
    g_mlp_pre_l = g_mlp_pre[l].reshape(1, D)
    g_mlp_post_l = g_mlp_post[l].reshape(1, D)
    g_subln_l = g_diff_subln[l].reshape(1, 2 * dh)
    lam_params = jnp.concatenate(
        [jnp.stack([lambda_q1[l], lambda_k1[l], lambda_q2[l], lambda_k2[l]]), jnp.zeros((4, dh), F32)], axis=0)
    lg = jnp.stack([ret_gamma_logit_fwd[l], ret_gamma_logit_bwd[l]], axis=1)
    lg = jnp.concatenate([lg, jnp.zeros((H, 6), F32)], axis=1)
    lg = jnp.broadcast_to(lg[:, :, None], (H, 8, V7X_LANES))

    n_cond = 16
    cond = jnp.concatenate([c, jnp.broadcast_to(c_ctx[None, :], (n_cond - Bs, D))], axis=0)
    mod = _adaln(cond, w_ada[l], b_ada[l].reshape(1, N_MOD * D)).reshape(n_cond, N_MOD, D)

    ret_offs = dict(off_q=0, off_k=ret_qk_w, off_v=2 * ret_qk_w, off_g=2 * ret_qk_w + ret_v_w)

    xs2 = x_sample.reshape(Bs * Ts, D)
    hs = _prenorm(xs2, mod, g_mix_pre_l, rows_per_group=Ts)
    sa = _proj(hs, w_in_l, (0, o_dq), out_dtype=BF16, tm=2048, name="proj_ret").reshape(Bs, Ts, -1)
    sqk, w_ret_o_b, w_diff_o_b, w_out_b = _proj(
        hs, w_in_l, (o_dq, o_dv), out_dtype=BF16, n_q_cols=diff_w, q_mult=q_mult,
        rope_tables=_rope_tables(Ts, dh // 2), cast_params=(w_ret_o[l], w_diff_o[l], w_out[l]),
        name="proj_qk_rope")
    sqk = sqk.reshape(Bs, Ts, 2 * diff_w)
    svg = _proj(hs, w_in_l, (o_dv, in_w), out_dtype=BF16, tm=2048, name="proj_vg")
    (ret_g,) = _retention(sa, lg, (state_ret_fwd, state_ret_bwd), emit_state=False, dk=dk, dv=dv, **ret_offs)
    caches = (cache_k[:, l].reshape(Bs, -1, diff_w), cache_v[:, l].reshape(Bs, -1, diff_w))
    diff_n = _diffattn((sqk, 0), (sqk, diff_w), (svg.reshape(Bs, Ts, -1), 0),
                       lam_params, g_subln_l, caches, lam_init=lam_init, dh=dh, heads_per_step=1)
    merged, w_up_b, w_down_b = _merge_a(
        ret_g.reshape(Bs * Ts, ret_v_w), diff_n.reshape(Bs * Ts, diff_w), w_ret_o_b, w_diff_o_b, svg,
        off_gr=diff_w, off_gd=diff_w + D, cast_params=(w_mlp_up[l], w_mlp_down[l]))

    def tail(merged, x2, B, T, rows_per_group):
        x1, h2 = _merge_b(merged, w_out_b, x2, mod, g_mix_post_l, g_mlp_pre_l, rows_per_group=rows_per_group)
        y = _mlp(h2, w_up_b, w_down_b, x1, mod, g_mlp_post_l, rows_per_group=rows_per_group)
        return y.reshape(B, T, D)

    y_sample = tail(merged, xs2, Bs, Ts, Ts)

    xp2 = x_prompt.reshape(Bp * Tp, D)
    hp = _prenorm(xp2, mod, g_mix_pre_l, rows_per_group=None)
    pa = _proj(hp, w_in_l, (0, o_dq), out_dtype=BF16, tm=2048, name="proj_ret").reshape(Bp, Tp, -1)
    pq = _proj(hp, w_in_l, (o_dq, o_dk), out_dtype=BF16, n_q_cols=diff_w, q_mult=q_mult,
               name="proj_q").reshape(Bp, Tp, diff_w)
    pk = _proj(hp, w_in_l, (o_dk, o_dv), out_dtype=F32, name="proj_k")
    pv = _proj(hp, w_in_l, (o_dv, o_gate), out_dtype=F32, name="proj_v")
    pg = _proj(hp, w_in_l, (o_gate, in_w), out_dtype=BF16, tm=2048, name="proj_gates")
    ret_g, new_state_fwd, new_state_bwd = _retention(pa, lg, None, emit_state=True, dk=dk, dv=dv,
                                                     batch_block=4, **ret_offs)
    diff_n = _diffattn((pq, 0), (pk.reshape(Bp, Tp, diff_w), 0), (pv.reshape(Bp, Tp, diff_w), 0),
                       lam_params, g_subln_l, None, lam_init=lam_init, dh=dh, heads_per_step=N_DIFF_HEADS)
    merged = _merge_a(ret_g.reshape(Bp * Tp, ret_v_w), diff_n.reshape(Bp * Tp, diff_w), w_ret_o_b, w_diff_o_b, pg,
                      off_gr=0, off_gd=D)
    y_prompt = tail(merged, xp2, Bp, Tp, None)
    new_cache_k = pk.reshape(Bp, 1, Tp, N_DIFF_HEADS, 2, dh)
    new_cache_v = pv.reshape(Bp, 1, Tp, N_DIFF_HEADS, 2 * dh)

    return (y_prompt, y_sample, new_cache_k, new_cache_v, new_state_fwd, new_state_bwd)
```

```python
import functools
import math

import jax
import jax.numpy as jnp
from jax import lax
from jax.experimental import pallas as pl
from jax.experimental.pallas import tpu as pltpu

F32 = jnp.float32
BF16 = jnp.bfloat16

N_RET_HEADS = 8
N_DIFF_HEADS = 8
N_MOD = 6
GRID_W = 64
ROPE_BASE = 10000.0
EPS = 1e-6
SUBLN_EPS = 1e-5

V7X_LANES = 128
V7X_VMEM_BYTES = 64 * 1024 * 1024
VMEM_RESERVE_BYTES = 8 * 1024 * 1024


def _vmem_limit(block_bytes, scratch_bytes=0, temp_bytes=0):
    want = 2 * block_bytes + scratch_bytes + temp_bytes + VMEM_RESERVE_BYTES
    return int(min(want, V7X_VMEM_BYTES - VMEM_RESERVE_BYTES // 2))


def _nbytes(shape, dtype):
    return math.prod(shape) * jnp.dtype(dtype).itemsize


def _mod_map(n_groups, rows_per_group, tm):
    if rows_per_group is None:
        return lambda i, *_: (n_groups - 1, 0, 0)
    return lambda i, *_: ((i * tm) // rows_per_group, 0, 0)


def _adaln_kernel(c_ref, w_ref, b_ref, o_ref):
    c = c_ref[...]
    s = (c * jax.nn.sigmoid(c)).astype(BF16)
    o_ref[...] = jnp.dot(s, w_ref[...].astype(BF16), preferred_element_type=F32) + b_ref[...]


def _adaln(cond, w_ada, b_ada, tn=512):
    R, D = cond.shape
    N = w_ada.shape[1]
    blocks = _nbytes((R, D), F32) + _nbytes((D, tn), F32) + _nbytes((1, tn), F32) + _nbytes((R, tn), F32)
    return pl.pallas_call(
        _adaln_kernel,
        out_shape=jax.ShapeDtypeStruct((R, N), F32),
        grid=(N // tn,),
        in_specs=[pl.BlockSpec((R, D), lambda j: (0, 0)),
                  pl.BlockSpec((D, tn), lambda j: (0, j)),
                  pl.BlockSpec((1, tn), lambda j: (0, j))],
        out_specs=pl.BlockSpec((R, tn), lambda j: (0, j)),
        compiler_params=pltpu.CompilerParams(
            dimension_semantics=("arbitrary",),
            vmem_limit_bytes=_vmem_limit(blocks, temp_bytes=_nbytes((D, tn), BF16))),
        name="adaln",
    )(cond, w_ada, b_ada)


ROW_CHUNK = 64
NORM_UNROLL = 4


def _norm_modulate(x_ref, gain, shift, scale, h_ref):
    a = gain * (1.0 + scale)
    n = x_ref.shape[0] // ROW_CHUNK

    def body(r, carry):
        sl = pl.ds(pl.multiple_of(r * ROW_CHUNK, ROW_CHUNK), ROW_CHUNK)
        x = x_ref[sl, :]
        ms = jnp.mean(x * x, axis=-1, keepdims=True)
        h_ref[sl, :] = (x * lax.rsqrt(ms + EPS) * a + shift).astype(h_ref.dtype)
        return carry

    lax.fori_loop(0, n, body, 0, unroll=NORM_UNROLL)


def _prenorm_kernel(x_ref, mod_ref, g_ref, h_ref):
    _norm_modulate(x_ref, g_ref[...], mod_ref[0:1, :], mod_ref[1:2, :], h_ref)


def _prenorm(x2, mod, gain, *, rows_per_group, tm=1024):
    M, D = x2.shape
    blocks = _nbytes((tm, D), F32) + _nbytes((tm, D), BF16) + _nbytes((N_MOD + 1, D), F32)
    return pl.pallas_call(
        _prenorm_kernel,
        out_shape=jax.ShapeDtypeStruct((M, D), BF16),
        grid=(M // tm,),
        in_specs=[pl.BlockSpec((tm, D), lambda i: (i, 0)),
                  pl.BlockSpec((None, N_MOD, D), _mod_map(mod.shape[0], rows_per_group, tm)),
                  pl.BlockSpec((1, D), lambda i: (0, 0))],
        out_specs=pl.BlockSpec((tm, D), lambda i: (i, 0)),
        compiler_params=pltpu.CompilerParams(
            dimension_semantics=("parallel",),
            vmem_limit_bytes=_vmem_limit(blocks, temp_bytes=4 * _nbytes((ROW_CHUNK, D), F32))),
        name="prenorm",
    )(x2, mod, gain)


CAST_ROWS = 256
ROPE_GROUP = V7X_LANES // 4


def _pair_rotary_lanes(x):
    n = x.shape[-1]
    group = (lax.broadcasted_iota(jnp.int32, x.shape, x.ndim - 1) % V7X_LANES) // ROPE_GROUP
    from_next = pltpu.roll(x, n - ROPE_GROUP, x.ndim - 1)
    from_prev = pltpu.roll(x, ROPE_GROUP, x.ndim - 1)
    return jnp.where(group == 1, from_next, jnp.where(group == 2, from_prev, x))


def _slab_cast_specs(params, n_steps, step_index):
    in_specs, out_specs, out_shapes, nbytes = [], [], [], 0
    for p in params:
        rows, width = p.shape[0] // n_steps, p.shape[1]
        assert rows * n_steps == p.shape[0] and rows % 16 == 0
        spec = pl.BlockSpec((rows, width), lambda *ids: (step_index(*ids), 0))
        in_specs.append(spec)
        out_specs.append(spec)
        out_shapes.append(jax.ShapeDtypeStruct(p.shape, BF16))
        nbytes += _nbytes((rows, width), F32) + _nbytes((rows, width), BF16)
    return in_specs, out_specs, out_shapes, nbytes


def _proj_kernel(h_ref, w_ref, *rest, n_q_blocks, q_mult, rope, n_cast):
    rest = list(rest)
    if rope:
        cos_ref, sin_ref = rest[:2]
        rest = rest[2:]
    cast_in, o_ref, cast_out, wb_ref = rest[:n_cast], rest[n_cast], rest[n_cast + 1:2 * n_cast + 1], rest[-1]
    for src, dst in zip(cast_in, cast_out):
        dst[...] = src[...].astype(dst.dtype)

    @pl.when(pl.program_id(1) == 0)
    def _():
        def body(r, carry):
            sl = pl.ds(pl.multiple_of(r * CAST_ROWS, CAST_ROWS), CAST_ROWS)
            wf = w_ref[sl, :]
            wb_ref[sl, :] = (_pair_rotary_lanes(wf) if rope else wf).astype(wb_ref.dtype)
            return carry

        lax.fori_loop(0, w_ref.shape[0] // CAST_ROWS, body, 0)

    mult = None
    if n_q_blocks:
        mult = jnp.where(pl.program_id(0) < n_q_blocks, q_mult, 1.0).astype(F32)
    acc = jnp.dot(h_ref[...], wb_ref[...], preferred_element_type=F32)
    if rope:
        cos, sin = cos_ref[...] * mult, sin_ref[...] * mult
        for c in range(acc.shape[1] // V7X_LANES):
            cs = slice(c * V7X_LANES, (c + 1) * V7X_LANES)
            xc = acc[:, cs]
            o_ref[:, cs] = (xc * cos + pltpu.roll(xc, V7X_LANES // 2, 1) * sin).astype(o_ref.dtype)
    elif mult is not None:
        o_ref[...] = (acc * mult).astype(o_ref.dtype)
    else:
        o_ref[...] = acc.astype(o_ref.dtype)


def _proj(h, w, cols, *, out_dtype, n_q_cols=0, q_mult=1.0, rope_tables=None, cast_params=(),
          tm=1024, tn=1024, name="proj"):
    M, D = h.shape
    N = cols[1] - cols[0]
    tm = min(tm, M)
    tn = min(tn, N)
    j0 = cols[0] // tn
    in_specs = [pl.BlockSpec((tm, D), lambda j, i: (i, 0)),
                pl.BlockSpec((D, tn), lambda j, i: (0, j0 + j))]
    args = [h, w]
    blocks = _nbytes((tm, D), BF16) + _nbytes((D, tn), w.dtype) + _nbytes((tm, tn), out_dtype)
    if rope_tables is not None:
        nt = rope_tables[0].shape[0] // tm
        for t in rope_tables:
            in_specs.append(pl.BlockSpec((tm, V7X_LANES), lambda j, i: (i % nt, 0)))
            args.append(t)
        blocks += len(rope_tables) * _nbytes((tm, V7X_LANES), F32)
    n_i = M // tm
    c_in, c_out, c_shapes, c_bytes = _slab_cast_specs(cast_params, (N // tn) * n_i, lambda j, i: j * n_i + i)
    outs = pl.pallas_call(
        functools.partial(_proj_kernel, n_q_blocks=n_q_cols // tn, q_mult=q_mult, rope=rope_tables is not None,
                          n_cast=len(cast_params)),
        out_shape=[jax.ShapeDtypeStruct((M, N), out_dtype)] + c_shapes,
        grid=(N // tn, n_i),
        in_specs=in_specs + c_in,
        out_specs=[pl.BlockSpec((tm, tn), lambda j, i: (i, j))] + c_out,
        scratch_shapes=[pltpu.VMEM((D, tn), BF16)],
        compiler_params=pltpu.CompilerParams(
            dimension_semantics=("parallel", "arbitrary"),
            vmem_limit_bytes=_vmem_limit(blocks + c_bytes, _nbytes((D, tn), BF16), _nbytes((tm, tn), F32))),
        name=name,
    )(*args, *cast_params)
    return outs if cast_params else outs[0]


RET_CHUNK = 256


def _log_sigmoid(x):
    return jnp.minimum(x, 0.0) - jnp.log1p(jnp.exp(-jnp.abs(x)))


def _ret_kernel(lg_ref, q_ref, k_ref, v_ref, g_ref, *rest, has_state, emit_state, k_scale):
    rest = list(rest)
    s0f_ref = s0b_ref = sf_out = sb_out = None
    if has_state:
        s0f_ref, s0b_ref = rest[:2]
        rest = rest[2:]
    o_ref = rest[0]
    rest = rest[1:]
    if emit_state:
        sf_out, sb_out = rest[:2]
        rest = rest[2:]
    (st_scr,) = rest

    bb, T, dk = q_ref.shape
    C = min(RET_CHUNK, T)
    n = T // C

    lsig = _log_sigmoid(lg_ref[...])
    lgf = lsig[0:1, 0:1]
    lgb = lsig[1:2, 0:1]
    ri = lax.broadcasted_iota(jnp.int32, (C, C), 0)
    ci = lax.broadcasted_iota(jnp.int32, (C, C), 1)
    rel = (ri - ci).astype(F32)
    decay = jnp.where(rel > 0, jnp.exp(rel * lgf), jnp.where(rel < 0, jnp.exp(-rel * lgb), 2.0)) * k_scale
    li = lax.broadcasted_iota(jnp.int32, (C, dk), 0).astype(F32)
    xi_f = jnp.exp((li + 1.0) * lgf)
    xi_b = jnp.exp((C - li) * lgb)
    zeta_f = jnp.exp((C - 1.0 - li) * lgf) * k_scale
    zeta_b = jnp.exp(li * lgb) * k_scale
    gc_f = jnp.exp(C * lgf)
    gc_b = jnp.exp(C * lgb)

    def rows(c):
        return slice(c * C, (c + 1) * C)

    def fwd_state_used(c):
        return has_state or c > 0

    def bwd_state_used(c):
        return has_state or c < n - 1

    def kv_pair(item):
        b, t = item
        out = []
        for c, zeta in ((t, zeta_f), (n - 1 - t, zeta_b)):
            ks = (k_ref[b, rows(c), :].astype(F32) * zeta).astype(BF16)
            out.append(lax.dot_general(ks, v_ref[b, rows(c), :], (((0,), (0,)), ((), ())),
                                       preferred_element_type=F32))
        return out

    n_upd = n if emit_state else n - 1
    items = [(b, t) for b in range(bb) for t in range(n_upd)]
    kv_next = kv_pair(items[0]) if items else None
    pos = 0
    for b in range(bb):
        sf = s0f_ref[b] if has_state else None
        sb = s0b_ref[b] if has_state else None
        for t in range(n):
            cf, cb = t, n - 1 - t
            if fwd_state_used(cf):
                st_scr[b, cf, 0:dk, :] = sf.astype(BF16)
            if bwd_state_used(cb):
                st_scr[b, cb, dk:2 * dk, :] = sb.astype(BF16)
            if t < n_upd:
                kvf, kvb = kv_next
                pos += 1
                if pos < len(items):
                    kv_next = kv_pair(items[pos])
                sf = kvf if sf is None else gc_f * sf + kvf
                sb = kvb if sb is None else gc_b * sb + kvb
        if emit_state:
            sf_out[b] = sf
            sb_out[b] = sb

    def chunk_out(item):
        b, c = item
        qc = q_ref[b, rows(c), :]
        s = lax.dot_general(qc, k_ref[b, rows(c), :], (((1,), (1,)), ((), ())),
                            preferred_element_type=F32) * decay
        o = jnp.dot(s.astype(BF16), v_ref[b, rows(c), :], preferred_element_type=F32)
        qf = qc.astype(F32)
        parts = []
        if fwd_state_used(c):
            parts.append(((qf * xi_f).astype(BF16), st_scr[b, c, 0:dk, :]))
        if bwd_state_used(c):
            parts.append(((qf * xi_b).astype(BF16), st_scr[b, c, dk:2 * dk, :]))
        if len(parts) == 2:
            o = o + jnp.dot(jnp.concatenate([parts[0][0], parts[1][0]], axis=1), st_scr[b, c],
                            preferred_element_type=F32)
        elif parts:
            o = o + jnp.dot(parts[0][0], parts[0][1], preferred_element_type=F32)
        return o

    items = [(b, c) for b in range(bb) for c in range(n)]
    o_next = chunk_out(items[0])
    for pos, (b, c) in enumerate(items):
        o = o_next
        if pos + 1 < len(items):
            o_next = chunk_out(items[pos + 1])
        on = o * lax.rsqrt(jnp.mean(o * o, axis=-1, keepdims=True) + EPS)
        hg = g_ref[b, rows(c), :] * 0.5
        o_ref[b, rows(c), :] = (hg + hg * jnp.tanh(hg)) * on.astype(o_ref.dtype)


def _retention(pa3, lg, states, *, emit_state, dk, dv, off_q, off_k, off_v, off_g, batch_block=1):
    B, T, _ = pa3.shape
    H = N_RET_HEADS
    bb = batch_block
    has_state = states is not None
    C = min(RET_CHUNK, T)
    n = T // C
    in_specs = [pl.BlockSpec((None, 8, V7X_LANES), lambda b, h: (h, 0, 0)),
                pl.BlockSpec((bb, T, dk), lambda b, h: (b, 0, off_q // dk + h)),
                pl.BlockSpec((bb, T, dk), lambda b, h: (b, 0, off_k // dk + h)),
                pl.BlockSpec((bb, T, dv), lambda b, h: (b, 0, off_v // dv + h)),
                pl.BlockSpec((bb, T, dv), lambda b, h: (b, 0, off_g // dv + h))]
    args = [lg, pa3, pa3, pa3, pa3]
    blocks = bb * (2 * _nbytes((T, dk), BF16) + 3 * _nbytes((T, dv), BF16)) + _nbytes((8, V7X_LANES), F32)
    state_spec = pl.BlockSpec((bb, None, None, dk, dv), lambda b, h: (b, 0, h, 0, 0))
    if has_state:
        in_specs += [state_spec, state_spec]
        args += list(states)
        blocks += 2 * bb * _nbytes((dk, dv), F32)
    out_shape = [jax.ShapeDtypeStruct((B, T, H * dv), BF16)]
    out_specs = [pl.BlockSpec((bb, T, dv), lambda b, h: (b, 0, h))]
    if emit_state:
        out_shape += [jax.ShapeDtypeStruct((B, 1, H, dk, dv), F32)] * 2
        out_specs += [state_spec, state_spec]
        blocks += 2 * bb * _nbytes((dk, dv), F32)
    scratch_bytes = _nbytes((bb, n, 2 * dk, dv), BF16)
    temps = 8 * _nbytes((C, dv), F32) + 8 * _nbytes((dk, dv), F32) + 4 * _nbytes((C, C), F32)
    return pl.pallas_call(
        functools.partial(_ret_kernel, has_state=has_state, emit_state=emit_state, k_scale=float(dk) ** -0.5),
        out_shape=out_shape,
        grid=(B // bb, H),
        in_specs=in_specs,
        out_specs=out_specs,
        scratch_shapes=[pltpu.VMEM((bb, n, 2 * dk, dv), BF16)],
        compiler_params=pltpu.CompilerParams(
            dimension_semantics=("parallel", "parallel"),
            vmem_limit_bytes=_vmem_limit(blocks, scratch_bytes, temps)),
        name="retention",
    )(*args)


DIFF_SUB_ROWS = 256


def _diff_kernel(lam_ref, gs_ref, q_ref, k_ref, v_ref, *rest, has_cache, lam_init, dh, sub):
    if has_cache:
        ck_ref, cv_ref, o_ref, kall, vall = rest
    else:
        o_ref, kall, vall = rest
    T = k_ref.shape[0]
    w = 2 * dh
    heads = q_ref.shape[1] // w

    kall[0:T, :] = k_ref[...].astype(kall.dtype)
    vall[0:T, :] = v_ref[...].astype(vall.dtype)
    if has_cache:
        kall[T:, :] = _pair_rotary_lanes(ck_ref[...]).astype(kall.dtype)
        vall[T:, :] = cv_ref[...].astype(vall.dtype)

    lp = lam_ref[...]
    lam = (jnp.exp(jnp.sum(lp[0:1, :] * lp[1:2, :], axis=-1, keepdims=True))
           - jnp.exp(jnp.sum(lp[2:3, :] * lp[3:4, :], axis=-1, keepdims=True)) + lam_init)
    gain = gs_ref[...] * (1.0 - lam_init)
    items = [(hh, slice(r0, r0 + sub)) for hh in range(heads) for r0 in range(0, T, sub)]

    def scores(item):
        hh, r = item
        out = []
        for m in range(2):
            cs = slice(hh * w + m * dh, hh * w + (m + 1) * dh)
            out.append(lax.dot_general(q_ref[r, cs], kall[:, cs],
                                       (((1,), (1,)), ((), ())), preferred_element_type=F32))
        return out

    s_next = scores(items[0])
    for t, (hh, r) in enumerate(items):
        s_cur = s_next
        if t + 1 < len(items):
            s_next = scores(items[t + 1])
        probs = []
        for s in s_cur:
            e = jnp.exp2(s - jnp.max(s, axis=-1, keepdims=True))
            probs.append((e, jnp.sum(e, axis=-1, keepdims=True)))
        (e1, l1), (e2, l2) = probs
        r1 = 1.0 / l1
        a = (e1 - e2 * (lam * l1 / l2)).astype(BF16)
        o = jnp.dot(a, vall[:, hh * w:(hh + 1) * w], preferred_element_type=F32) * r1
        on = o * lax.rsqrt(jnp.mean(o * o, axis=-1, keepdims=True) + SUBLN_EPS)
        o_ref[r, hh * w:(hh + 1) * w] = (on * gain).astype(o_ref.dtype)


def _diffattn(q_src, k_src, v_src, lam_params, g_subln, caches, *, lam_init, dh, heads_per_step):
    (qa, off_q), (ka, off_k), (va, off_v) = q_src, k_src, v_src
    B, T, _ = qa.shape
    H = N_DIFF_HEADS
    w = 2 * dh
    gw = heads_per_step * w
    sub = min(DIFF_SUB_ROWS, T)
    has_cache = caches is not None
    Tk = T + (caches[0].shape[1] if has_cache else 0)
    in_specs = [pl.BlockSpec((8, dh), lambda b, g: (0, 0)),
                pl.BlockSpec((1, w), lambda b, g: (0, 0)),
                pl.BlockSpec((None, T, gw), lambda b, g: (b, 0, off_q // gw + g)),
                pl.BlockSpec((None, T, gw), lambda b, g: (b, 0, off_k // gw + g)),
                pl.BlockSpec((None, T, gw), lambda b, g: (b, 0, off_v // gw + g))]
    args = [lam_params, g_subln, qa, ka, va]
    blocks = 2 * _nbytes((T, gw), BF16) + _nbytes((T, gw), ka.dtype) + _nbytes((T, gw), va.dtype)
    if has_cache:
        P = caches[0].shape[1]
        cspec = pl.BlockSpec((None, P, gw), lambda b, g: (b, 0, g))
        in_specs += [cspec, cspec]
        args += list(caches)
        blocks += 2 * _nbytes((P, gw), F32)
    temps = 10 * _nbytes((sub, Tk), F32)
    scratch_rows = [Tk, Tk]
    return pl.pallas_call(
        functools.partial(_diff_kernel, has_cache=has_cache, lam_init=lam_init, dh=dh, sub=sub),
        out_shape=jax.ShapeDtypeStruct((B, T, H * w), BF16),
        grid=(B, H // heads_per_step),
        in_specs=in_specs,
        out_specs=pl.BlockSpec((None, T, gw), lambda b, g: (b, 0, g)),
        scratch_shapes=[pltpu.VMEM((n, gw), BF16) for n in scratch_rows],
        compiler_params=pltpu.CompilerParams(
            dimension_semantics=("parallel", "parallel"),
            vmem_limit_bytes=_vmem_limit(blocks, sum(_nbytes((n, gw), BF16) for n in scratch_rows), temps)),
        name="diffattn",
    )(*args)


def _merge_a_kernel(r_ref, d_ref, wr_ref, wd_ref, gr_ref, gd_ref, *rest):
    n_cast = len(rest) // 2
    cast_in, o_ref, cast_out = rest[:n_cast], rest[n_cast], rest[n_cast + 1:]
    for src, dst in zip(cast_in, cast_out):
        dst[...] = src[...].astype(dst.dtype)
    rb = jnp.dot(r_ref[...], wr_ref[...], preferred_element_type=F32)
    db = jnp.dot(d_ref[...], wd_ref[...], preferred_element_type=F32)
    gr = jax.nn.sigmoid(gr_ref[...].astype(F32))
    gd = jax.nn.sigmoid(gd_ref[...].astype(F32))
    o_ref[...] = (gr * rb + gd * db).astype(o_ref.dtype)


def _merge_a(ret_g, diff_n, w_ret_o, w_diff_o, gates, *, off_gr, off_gd, cast_params=(), tm=1024, tn=512):
    M, Kr = ret_g.shape
    Kd = diff_n.shape[1]
    D = w_ret_o.shape[1]
    tm = min(tm, M)
    n_j = D // tn
    blocks = (_nbytes((tm, Kr), BF16) + _nbytes((tm, Kd), BF16) + _nbytes((Kr, tn), BF16)
              + _nbytes((Kd, tn), BF16) + 3 * _nbytes((tm, tn), BF16))
    c_in, c_out, c_shapes, c_bytes = _slab_cast_specs(cast_params, (M // tm) * n_j, lambda i, j: i * n_j + j)
    outs = pl.pallas_call(
        _merge_a_kernel,
        out_shape=[jax.ShapeDtypeStruct((M, D), BF16)] + c_shapes,
        grid=(M // tm, n_j),
        in_specs=[pl.BlockSpec((tm, Kr), lambda i, j: (i, 0)),
                  pl.BlockSpec((tm, Kd), lambda i, j: (i, 0)),
                  pl.BlockSpec((Kr, tn), lambda i, j: (0, j)),
                  pl.BlockSpec((Kd, tn), lambda i, j: (0, j)),
                  pl.BlockSpec((tm, tn), lambda i, j: (i, off_gr // tn + j)),
                  pl.BlockSpec((tm, tn), lambda i, j: (i, off_gd // tn + j))] + c_in,
        out_specs=[pl.BlockSpec((tm, tn), lambda i, j: (i, j))] + c_out,
        compiler_params=pltpu.CompilerParams(
            dimension_semantics=("parallel", "arbitrary"),
            vmem_limit_bytes=_vmem_limit(blocks + c_bytes, temp_bytes=4 * _nbytes((tm, tn), F32))),
        name="merge_a",
    )(ret_g, diff_n, w_ret_o, w_diff_o, gates, gates, *cast_params)
    return outs if cast_params else outs[0]


MERGE_SUB_ROWS = 256


def _merge_b_kernel(m_ref, w_ref, x_ref, mod_ref, gpost_ref, gpre_ref, x1_ref, h2_ref):
    gate_post = mod_ref[2:3, :] * gpost_ref[...]
    a2 = gpre_ref[...] * (1.0 + mod_ref[4:5, :])
    shift2 = mod_ref[3:4, :]
    sub = MERGE_SUB_ROWS
    n_sub = x_ref.shape[0] // sub

    def out_proj(k):
        return jnp.dot(m_ref[k * sub:(k + 1) * sub, :], w_ref[...], preferred_element_type=F32)

    y_next = out_proj(0)
    for k in range(n_sub):
        y_all = y_next
        if k + 1 < n_sub:
            y_next = out_proj(k + 1)
        for c in range(sub // ROW_CHUNK):
            sl = slice(k * sub + c * ROW_CHUNK, k * sub + (c + 1) * ROW_CHUNK)
            y = y_all[c * ROW_CHUNK:(c + 1) * ROW_CHUNK, :]
            x1 = x_ref[sl, :] + y * lax.rsqrt(jnp.mean(y * y, axis=-1, keepdims=True) + EPS) * gate_post
            x1_ref[sl, :] = x1
            h2_ref[sl, :] = (x1 * lax.rsqrt(jnp.mean(x1 * x1, axis=-1, keepdims=True) + EPS) * a2
                             + shift2).astype(h2_ref.dtype)


def _merge_b(merged, w_out, x2, mod, g_post, g_pre, *, rows_per_group, tm=512):
    M, D = x2.shape
    tm = min(tm, M)
    blocks = (_nbytes((tm, D), BF16) * 2 + _nbytes((D, D), BF16) + 2 * _nbytes((tm, D), F32)
              + _nbytes((N_MOD + 2, D), F32))
    return pl.pallas_call(
        _merge_b_kernel,
        out_shape=[jax.ShapeDtypeStruct((M, D), F32), jax.ShapeDtypeStruct((M, D), BF16)],
        grid=(M // tm,),
        in_specs=[pl.BlockSpec((tm, D), lambda i: (i, 0)),
                  pl.BlockSpec((D, D), lambda i: (0, 0)),
                  pl.BlockSpec((tm, D), lambda i: (i, 0)),
                  pl.BlockSpec((None, N_MOD, D), _mod_map(mod.shape[0], rows_per_group, tm)),
                  pl.BlockSpec((1, D), lambda i: (0, 0)),
                  pl.BlockSpec((1, D), lambda i: (0, 0))],
        out_specs=[pl.BlockSpec((tm, D), lambda i: (i, 0)),
                   pl.BlockSpec((tm, D), lambda i: (i, 0))],
        compiler_params=pltpu.CompilerParams(
            dimension_semantics=("parallel",),
            vmem_limit_bytes=_vmem_limit(blocks, temp_bytes=2 * _nbytes((tm, D), F32))),
        name="merge_b",
    )(merged, w_out, x2, mod, g_post, g_pre)


def _mlp_kernel(h_ref, wu_ref, wd_ref, x_ref, mod_ref, g_ref, o_ref):
    j = pl.program_id(1)
    last = pl.num_programs(1) - 1

    def part(rows):
        u = jnp.maximum(jnp.dot(h_ref[rows, :], wu_ref[...], preferred_element_type=F32), 0.0)
        return jnp.dot((u * u).astype(BF16), wd_ref[...], preferred_element_type=F32)

    @pl.when(j == 0)
    def _():
        o_ref[...] = part(slice(None))

    @pl.when(jnp.logical_and(j > 0, j < last))
    def _():
        o_ref[...] += part(slice(None))

    @pl.when(j == last)
    def _():
        gate_post = mod_ref[5:6, :] * g_ref[...]
        sub = MERGE_SUB_ROWS
        n_sub = x_ref.shape[0] // sub
        p_next = part(slice(0, sub))
        for k in range(n_sub):
            p_cur = p_next
            if k + 1 < n_sub:
                p_next = part(slice((k + 1) * sub, (k + 2) * sub))
            for c in range(sub // ROW_CHUNK):
                sl = slice(k * sub + c * ROW_CHUNK, k * sub + (c + 1) * ROW_CHUNK)
                y = o_ref[sl, :] + p_cur[c * ROW_CHUNK:(c + 1) * ROW_CHUNK, :]
                o_ref[sl, :] = (x_ref[sl, :]
                                + y * lax.rsqrt(jnp.mean(y * y, axis=-1, keepdims=True) + EPS) * gate_post)


def _mlp(h2, w_up, w_down, x1, mod, g_post, *, rows_per_group, tm=512, tf=1024):
    M, D = x1.shape
    Fh = w_up.shape[1]
    tm = min(tm, M)
    assert Fh // tf >= 2
    blocks = (_nbytes((tm, D), BF16) + 2 * _nbytes((D, tf), BF16) + 2 * _nbytes((tm, D), F32)
              + _nbytes((N_MOD + 1, D), F32))
    return pl.pallas_call(
        _mlp_kernel,
        out_shape=jax.ShapeDtypeStruct((M, D), F32),
        grid=(M // tm, Fh // tf),
        in_specs=[pl.BlockSpec((tm, D), lambda i, j: (i, 0)),
                  pl.BlockSpec((D, tf), lambda i, j: (0, j)),
                  pl.BlockSpec((tf, D), lambda i, j: (j, 0)),
                  pl.BlockSpec((tm, D), lambda i, j: (i, 0)),
                  pl.BlockSpec((None, N_MOD, D), _mod_map(mod.shape[0], rows_per_group, tm)),
                  pl.BlockSpec((1, D), lambda i, j: (0, 0))],
        out_specs=pl.BlockSpec((tm, D), lambda i, j: (i, 0)),
        compiler_params=pltpu.CompilerParams(
            dimension_semantics=("parallel", "arbitrary"),
            vmem_limit_bytes=_vmem_limit(blocks, temp_bytes=2 * _nbytes((tm, tf), F32))),
        name="mlp",
    )(h2, w_up, w_down, x1, mod, g_post)


def _rope_tables(T, rope_half):
    rows = T // GRID_W
    row = jnp.repeat(jnp.arange(rows, dtype=F32), GRID_W)
    col = jnp.tile(jnp.arange(GRID_W, dtype=F32), rows)
    inv = ROPE_BASE ** (-jnp.arange(0, rope_half, 2, dtype=F32) / rope_half)
    ar, ac = row[:, None] * inv, col[:, None] * inv
    cos = jnp.concatenate([jnp.cos(ar), jnp.cos(ac), jnp.cos(ar), jnp.cos(ac)], axis=-1)
    sin = jnp.concatenate([-jnp.sin(ar), -jnp.sin(ac), jnp.sin(ar), jnp.sin(ac)], axis=-1)
    return cos, sin


def kernel(x_prompt, x_sample, cache_k, cache_v, state_ret_fwd, state_ret_bwd, c, c_ctx, w_ada, b_ada, g_mix_pre, g_mix_post, g_mlp_pre, g_mlp_post, w_in, ret_gamma_logit_fwd, ret_gamma_logit_bwd, w_ret_o, lambda_q1, lambda_k1, lambda_q2, lambda_k2, g_diff_subln, w_diff_o, w_out, w_mlp_up, w_mlp_down):
    Bp, Tp, D = x_prompt.shape
    Bs, Ts, _ = x_sample.shape
    depth = w_in.shape[0]
    assert depth == 1
    l = 0
    H = N_RET_HEADS
    dk = D // H
    dv = 2 * dk
    dh = D // N_DIFF_HEADS // 2
    ret_qk_w, ret_v_w, diff_w = H * dk, H * dv, N_DIFF_HEADS * 2 * dh
    o_dq = 2 * ret_qk_w + 2 * ret_v_w
    o_dk, o_dv, o_gate = o_dq + diff_w, o_dq + 2 * diff_w, o_dq + 3 * diff_w
    in_w = o_gate + 2 * D
    assert w_in.shape[2] == in_w
    lam_init = 0.8 - 0.6 * math.exp(-0.3 * l)
    q_mult = float(dh) ** -0.5 * math.log2(math.e)

    w_in_l = w_in[l]
    g_mix_pre_l = g_mix_pre[l].reshape(1, D)
    g_mix_post_l = g_mix_post[l].reshape(1, D)
    g_mlp_pre_l = g_mlp_pre[l].reshape(1, D)
    g_mlp_post_l = g_mlp_post[l].reshape(1, D)
    g_subln_l = g_diff_subln[l].reshape(1, 2 * dh)
    lam_params = jnp.concatenate(
        [jnp.stack([lambda_q1[l], lambda_k1[l], lambda_q2[l], lambda_k2[l]]), jnp.zeros((4, dh), F32)], axis=0)
    lg = jnp.stack([ret_gamma_logit_fwd[l], ret_gamma_logit_bwd[l]], axis=1)
    lg = jnp.concatenate([lg, jnp.zeros((H, 6), F32)], axis=1)
    lg = jnp.broadcast_to(lg[:, :, None], (H, 8, V7X_LANES))

    n_cond = 16
    cond = jnp.concatenate([c, jnp.broadcast_to(c_ctx[None, :], (n_cond - Bs, D))], axis=0)
    mod = _adaln(cond, w_ada[l], b_ada[l].reshape(1, N_MOD * D)).reshape(n_cond, N_MOD, D)

    ret_offs = dict(off_q=0, off_k=ret_qk_w, off_v=2 * ret_qk_w, off_g=2 * ret_qk_w + ret_v_w)

    xs2 = x_sample.reshape(Bs * Ts, D)
    hs = _prenorm(xs2, mod, g_mix_pre_l, rows_per_group=Ts)
    sa = _proj(hs, w_in_l, (0, o_dq), out_dtype=BF16, tm=2048, name="proj_ret").reshape(Bs, Ts, -1)
    sqk, w_ret_o_b, w_diff_o_b, w_out_b = _proj(
        hs, w_in_l, (o_dq, o_dv), out_dtype=BF16, n_q_cols=diff_w, q_mult=q_mult,
        rope_tables=_rope_tables(Ts, dh // 2), cast_params=(w_ret_o[l], w_diff_o[l], w_out[l]),
        name="proj_qk_rope")
    sqk = sqk.reshape(Bs, Ts, 2 * diff_w)
    svg = _proj(hs, w_in_l, (o_dv, in_w), out_dtype=BF16, tm=2048, name="proj_vg")
    (ret_g,) = _retention(sa, lg, (state_ret_fwd, state_ret_bwd), emit_state=False, dk=dk, dv=dv, **ret_offs)
    caches = (cache_k[:, l].reshape(Bs, -1, diff_w), cache_v[:, l].reshape(Bs, -1, diff_w))
    diff_n = _diffattn((sqk, 0), (sqk, diff_w), (svg.reshape(Bs, Ts, -1), 0),
                       lam_params, g_subln_l, caches, lam_init=lam_init, dh=dh, heads_per_step=1)
    merged, w_up_b, w_down_b = _merge_a(
        ret_g.reshape(Bs * Ts, ret_v_w), diff_n.reshape(Bs * Ts, diff_w), w_ret_o_b, w_diff_o_b, svg,
        off_gr=diff_w, off_gd=diff_w + D, cast_params=(w_mlp_up[l], w_mlp_down[l]))

    def tail(merged, x2, B, T, rows_per_group):
        x1, h2 = _merge_b(merged, w_out_b, x2, mod, g_mix_post_l, g_mlp_pre_l, rows_per_group=rows_per_group)
        y = _mlp(h2, w_up_b, w_down_b, x1, mod, g_mlp_post_l, rows_per_group=rows_per_group)
        return y.reshape(B, T, D)

    y_sample = tail(merged, xs2, Bs, Ts, Ts)

    xp2 = x_prompt.reshape(Bp * Tp, D)
    hp = _prenorm(xp2, mod, g_mix_pre_l, rows_per_group=None)
    pa = _proj(hp, w_in_l, (0, o_dq), out_dtype=BF16, tm=2048, name="proj_ret").reshape(Bp, Tp, -1)
    pq = _proj(hp, w_in_l, (o_dq, o_dk), out_dtype=BF16, n_q_cols=diff_w, q_mult=q_mult,
               name="proj_q").reshape(Bp, Tp, diff_w)
    pk = _proj(hp, w_in_l, (o_dk, o_dv), out_dtype=F32, name="proj_k")
    pv = _proj(hp, w_in_l, (o_dv, o_gate), out_dtype=F32, name="proj_v")
    pg = _proj(hp, w_in_l, (o_gate, in_w), out_dtype=BF16, tm=2048, name="proj_gates")
    ret_g, new_state_fwd, new_state_bwd = _retention(pa, lg, None, emit_state=True, dk=dk, dv=dv,
                                                     batch_block=4, **ret_offs)
    diff_n = _diffattn((pq, 0), (pk.reshape(Bp, Tp, diff_w), 0), (pv.reshape(Bp, Tp, diff_w), 0),
                       lam_params, g_subln_l, None, lam_init=lam_init, dh=dh, heads_per_step=N_DIFF_HEADS)
    merged = _merge_a(ret_g.reshape(Bp * Tp, ret_v_w), diff_n.reshape(Bp * Tp, diff_w), w_ret_o_b, w_diff_o_b, pg,
                      off_gr=0, off_gd=D)
    y_prompt = tail(merged, xp2, Bp, Tp, None)
    new_cache_k = pk.reshape(Bp, 1, Tp, N_DIFF_HEADS, 2, dh)
    new_cache_v = pv.reshape(Bp, 1, Tp, N_DIFF_HEADS, 2 * dh)

    return (y_prompt, y_sample, new_cache_k, new_cache_v, new_state_fwd, new_state_bwd)
```

```python
import functools
import math

import jax
import jax.numpy as jnp
from jax import lax
from jax.experimental import pallas as pl
from jax.experimental.pallas import tpu as pltpu

F32 = jnp.float32
BF16 = jnp.bfloat16

N_RET_HEADS = 8
N_DIFF_HEADS = 8
N_MOD = 6
GRID_W = 64
ROPE_BASE = 10000.0
EPS = 1e-6
SUBLN_EPS = 1e-5

V7X_LANES = 128
V7X_VMEM_BYTES = 64 * 1024 * 1024
VMEM_RESERVE_BYTES = 8 * 1024 * 1024


def _vmem_limit(block_bytes, scratch_bytes=0, temp_bytes=0):
    want = 2 * block_bytes + scratch_bytes + temp_bytes + VMEM_RESERVE_BYTES
    return int(min(want, V7X_VMEM_BYTES - VMEM_RESERVE_BYTES // 2))


def _nbytes(shape, dtype):
    return math.prod(shape) * jnp.dtype(dtype).itemsize


def _mod_map(n_groups, rows_per_group, tm):
    if rows_per_group is None:
        return lambda i, *_: (n_groups - 1, 0, 0)
    return lambda i, *_: ((i * tm) // rows_per_group, 0, 0)


def _adaln_kernel(c_ref, w_ref, b_ref, o_ref):
    c = c_ref[...]
    s = (c * jax.nn.sigmoid(c)).astype(BF16)
    o_ref[...] = jnp.dot(s, w_ref[...].astype(BF16), preferred_element_type=F32) + b_ref[...]


def _adaln(cond, w_ada, b_ada, tn=512):
    R, D = cond.shape
    N = w_ada.shape[1]
    blocks = _nbytes((R, D), F32) + _nbytes((D, tn), F32) + _nbytes((1, tn), F32) + _nbytes((R, tn), F32)
    return pl.pallas_call(
        _adaln_kernel,
        out_shape=jax.ShapeDtypeStruct((R, N), F32),
        grid=(N // tn,),
        in_specs=[pl.BlockSpec((R, D), lambda j: (0, 0)),
                  pl.BlockSpec((D, tn), lambda j: (0, j)),
                  pl.BlockSpec((1, tn), lambda j: (0, j))],
        out_specs=pl.BlockSpec((R, tn), lambda j: (0, j)),
        compiler_params=pltpu.CompilerParams(
            dimension_semantics=("arbitrary",),
            vmem_limit_bytes=_vmem_limit(blocks, temp_bytes=_nbytes((D, tn), BF16))),
        name="adaln",
    )(cond, w_ada, b_ada)


ROW_CHUNK = 64
NORM_UNROLL = 4


def _norm_modulate(x_ref, gain, shift, scale, h_ref):
    a = gain * (1.0 + scale)
    n = x_ref.shape[0] // ROW_CHUNK

    def body(r, carry):
        sl = pl.ds(pl.multiple_of(r * ROW_CHUNK, ROW_CHUNK), ROW_CHUNK)
        x = x_ref[sl, :]
        ms = jnp.mean(x * x, axis=-1, keepdims=True)
        h_ref[sl, :] = (x * lax.rsqrt(ms + EPS) * a + shift).astype(h_ref.dtype)
        return carry

    lax.fori_loop(0, n, body, 0, unroll=NORM_UNROLL)


def _prenorm_kernel(x_ref, mod_ref, g_ref, h_ref):
    _norm_modulate(x_ref, g_ref[...], mod_ref[0:1, :], mod_ref[1:2, :], h_ref)


def _prenorm(x2, mod, gain, *, rows_per_group, tm=1024):
    M, D = x2.shape
    blocks = _nbytes((tm, D), F32) + _nbytes((tm, D), BF16) + _nbytes((N_MOD + 1, D), F32)
    return pl.pallas_call(
        _prenorm_kernel,
        out_shape=jax.ShapeDtypeStruct((M, D), BF16),
        grid=(M // tm,),
        in_specs=[pl.BlockSpec((tm, D), lambda i: (i, 0)),
                  pl.BlockSpec((None, N_MOD, D), _mod_map(mod.shape[0], rows_per_group, tm)),
                  pl.BlockSpec((1, D), lambda i: (0, 0))],
        out_specs=pl.BlockSpec((tm, D), lambda i: (i, 0)),
        compiler_params=pltpu.CompilerParams(
            dimension_semantics=("parallel",),
            vmem_limit_bytes=_vmem_limit(blocks, temp_bytes=4 * _nbytes((ROW_CHUNK, D), F32))),
        name="prenorm",
    )(x2, mod, gain)


CAST_ROWS = 256
ROPE_GROUP = V7X_LANES // 4


def _pair_rotary_lanes(x):
    n = x.shape[-1]
    group = (lax.broadcasted_iota(jnp.int32, x.shape, x.ndim - 1) % V7X_LANES) // ROPE_GROUP
    from_next = pltpu.roll(x, n - ROPE_GROUP, x.ndim - 1)
    from_prev = pltpu.roll(x, ROPE_GROUP, x.ndim - 1)
    return jnp.where(group == 1, from_next, jnp.where(group == 2, from_prev, x))


def _slab_cast_specs(params, n_steps, step_index):
    in_specs, out_specs, out_shapes, nbytes = [], [], [], 0
    for p in params:
        rows, width = p.shape[0] // n_steps, p.shape[1]
        assert rows * n_steps == p.shape[0] and rows % 16 == 0
        spec = pl.BlockSpec((rows, width), lambda *ids: (step_index(*ids), 0))
        in_specs.append(spec)
        out_specs.append(spec)
        out_shapes.append(jax.ShapeDtypeStruct(p.shape, BF16))
        nbytes += _nbytes((rows, width), F32) + _nbytes((rows, width), BF16)
    return in_specs, out_specs, out_shapes, nbytes


def _proj_kernel(h_ref, w_ref, *rest, n_q_blocks, q_mult, rope, n_cast):
    rest = list(rest)
    if rope:
        cos_ref, sin_ref = rest[:2]
        rest = rest[2:]
    cast_in, o_ref, cast_out, wb_ref = rest[:n_cast], rest[n_cast], rest[n_cast + 1:2 * n_cast + 1], rest[-1]
    for src, dst in zip(cast_in, cast_out):
        dst[...] = src[...].astype(dst.dtype)

    @pl.when(pl.program_id(1) == 0)
    def _():
        def body(r, carry):
            sl = pl.ds(pl.multiple_of(r * CAST_ROWS, CAST_ROWS), CAST_ROWS)
            wf = w_ref[sl, :]
            wb_ref[sl, :] = (_pair_rotary_lanes(wf) if rope else wf).astype(wb_ref.dtype)
            return carry

        lax.fori_loop(0, w_ref.shape[0] // CAST_ROWS, body, 0)

    mult = None
    if n_q_blocks:
        mult = jnp.where(pl.program_id(0) < n_q_blocks, q_mult, 1.0).astype(F32)
    acc = jnp.dot(h_ref[...], wb_ref[...], preferred_element_type=F32)
    if rope:
        cos, sin = cos_ref[...] * mult, sin_ref[...] * mult
        for c in range(acc.shape[1] // V7X_LANES):
            cs = slice(c * V7X_LANES, (c + 1) * V7X_LANES)
            xc = acc[:, cs]
            o_ref[:, cs] = (xc * cos + pltpu.roll(xc, V7X_LANES // 2, 1) * sin).astype(o_ref.dtype)
    elif mult is not None:
        o_ref[...] = (acc * mult).astype(o_ref.dtype)
    else:
        o_ref[...] = acc.astype(o_ref.dtype)


def _proj(h, w, cols, *, out_dtype, n_q_cols=0, q_mult=1.0, rope_tables=None, cast_params=(),
          tm=1024, tn=1024, name="proj"):
    M, D = h.shape
    N = cols[1] - cols[0]
    tm = min(tm, M)
    tn = min(tn, N)
    j0 = cols[0] // tn
    in_specs = [pl.BlockSpec((tm, D), lambda j, i: (i, 0)),
                pl.BlockSpec((D, tn), lambda j, i: (0, j0 + j))]
    args = [h, w]
    blocks = _nbytes((tm, D), BF16) + _nbytes((D, tn), w.dtype) + _nbytes((tm, tn), out_dtype)
    if rope_tables is not None:
        nt = rope_tables[0].shape[0] // tm
        for t in rope_tables:
            in_specs.append(pl.BlockSpec((tm, V7X_LANES), lambda j, i: (i % nt, 0)))
            args.append(t)
        blocks += len(rope_tables) * _nbytes((tm, V7X_LANES), F32)
    n_i = M // tm
    c_in, c_out, c_shapes, c_bytes = _slab_cast_specs(cast_params, (N // tn) * n_i, lambda j, i: j * n_i + i)
    outs = pl.pallas_call(
        functools.partial(_proj_kernel, n_q_blocks=n_q_cols // tn, q_mult=q_mult, rope=rope_tables is not None,
                          n_cast=len(cast_params)),
        out_shape=[jax.ShapeDtypeStruct((M, N), out_dtype)] + c_shapes,
        grid=(N // tn, n_i),
        in_specs=in_specs + c_in,
        out_specs=[pl.BlockSpec((tm, tn), lambda j, i: (i, j))] + c_out,
        scratch_shapes=[pltpu.VMEM((D, tn), BF16)],
        compiler_params=pltpu.CompilerParams(
            dimension_semantics=("parallel", "arbitrary"),
            vmem_limit_bytes=_vmem_limit(blocks + c_bytes, _nbytes((D, tn), BF16), _nbytes((tm, tn), F32))),
        name=name,
    )(*args, *cast_params)
    return outs if cast_params else outs[0]


RET_CHUNK = 256


def _log_sigmoid(x):
    return jnp.minimum(x, 0.0) - jnp.log1p(jnp.exp(-jnp.abs(x)))


def _ret_kernel(lg_ref, q_ref, k_ref, v_ref, g_ref, *rest, has_state, emit_state, k_scale):
    rest = list(rest)
    s0f_ref = s0b_ref = sf_out = sb_out = None
    if has_state:
        s0f_ref, s0b_ref = rest[:2]
        rest = rest[2:]
    o_ref = rest[0]
    rest = rest[1:]
    if emit_state:
        sf_out, sb_out = rest[:2]
        rest = rest[2:]
    (st_scr,) = rest

    bb, T, dk = q_ref.shape
    C = min(RET_CHUNK, T)
    n = T // C

    lsig = _log_sigmoid(lg_ref[...])
    lgf = lsig[0:1, 0:1]
    lgb = lsig[1:2, 0:1]
    ri = lax.broadcasted_iota(jnp.int32, (C, C), 0)
    ci = lax.broadcasted_iota(jnp.int32, (C, C), 1)
    rel = (ri - ci).astype(F32)
    decay = jnp.where(rel > 0, jnp.exp(rel * lgf), jnp.where(rel < 0, jnp.exp(-rel * lgb), 2.0)) * k_scale
    li = lax.broadcasted_iota(jnp.int32, (C, dk), 0).astype(F32)
    xi_f = jnp.exp((li + 1.0) * lgf)
    xi_b = jnp.exp((C - li) * lgb)
    zeta_f = jnp.exp((C - 1.0 - li) * lgf) * k_scale
    zeta_b = jnp.exp(li * lgb) * k_scale
    gc_f = jnp.exp(C * lgf)
    gc_b = jnp.exp(C * lgb)

    def rows(c):
        return slice(c * C, (c + 1) * C)

    def fwd_state_used(c):
        return has_state or c > 0

    def bwd_state_used(c):
        return has_state or c < n - 1

    def kv_pair(item):
        b, t = item
        out = []
        for c, zeta in ((t, zeta_f), (n - 1 - t, zeta_b)):
            ks = (k_ref[b, rows(c), :].astype(F32) * zeta).astype(BF16)
            out.append(lax.dot_general(ks, v_ref[b, rows(c), :], (((0,), (0,)), ((), ())),
                                       preferred_element_type=F32))
        return out

    n_upd = n if emit_state else n - 1
    items = [(b, t) for b in range(bb) for t in range(n_upd)]
    kv_next = kv_pair(items[0]) if items else None
    pos = 0
    for b in range(bb):
        sf = s0f_ref[b] if has_state else None
        sb = s0b_ref[b] if has_state else None
        for t in range(n):
            cf, cb = t, n - 1 - t
            if fwd_state_used(cf):
                st_scr[b, cf, 0:dk, :] = sf.astype(BF16)
            if bwd_state_used(cb):
                st_scr[b, cb, dk:2 * dk, :] = sb.astype(BF16)
            if t < n_upd:
                kvf, kvb = kv_next
                pos += 1
                if pos < len(items):
                    kv_next = kv_pair(items[pos])
                sf = kvf if sf is None else gc_f * sf + kvf
                sb = kvb if sb is None else gc_b * sb + kvb
        if emit_state:
            sf_out[b] = sf
            sb_out[b] = sb

    def chunk_out(item):
        b, c = item
        qc = q_ref[b, rows(c), :]
        s = lax.dot_general(qc, k_ref[b, rows(c), :], (((1,), (1,)), ((), ())),
                            preferred_element_type=F32) * decay
        o = jnp.dot(s.astype(BF16), v_ref[b, rows(c), :], preferred_element_type=F32)
        qf = qc.astype(F32)
        parts = []
        if fwd_state_used(c):
            parts.append(((qf * xi_f).astype(BF16), st_scr[b, c, 0:dk, :]))
        if bwd_state_used(c):
            parts.append(((qf * xi_b).astype(BF16), st_scr[b, c, dk:2 * dk, :]))
        if len(parts) == 2:
            o = o + jnp.dot(jnp.concatenate([parts[0][0], parts[1][0]], axis=1), st_scr[b, c],
                            preferred_element_type=F32)
        elif parts:
            o = o + jnp.dot(parts[0][0], parts[0][1], preferred_element_type=F32)
        return o

    items = [(b, c) for b in range(bb) for c in range(n)]
    o_next = chunk_out(items[0])
    for pos, (b, c) in enumerate(items):
        o = o_next
        if pos + 1 < len(items):
            o_next = chunk_out(items[pos + 1])
        on = o * lax.rsqrt(jnp.mean(o * o, axis=-1, keepdims=True) + EPS)
        hg = g_ref[b, rows(c), :] * 0.5
        o_ref[b, rows(c), :] = (hg + hg * jnp.tanh(hg)) * on.astype(o_ref.dtype)


def _retention(pa3, lg, states, *, emit_state, dk, dv, off_q, off_k, off_v, off_g, batch_block=1):
    B, T, _ = pa3.shape
    H = N_RET_HEADS
    bb = batch_block
    has_state = states is not None
    C = min(RET_CHUNK, T)
    n = T // C
    in_specs = [pl.BlockSpec((None, 8, V7X_LANES), lambda b, h: (h, 0, 0)),
                pl.BlockSpec((bb, T, dk), lambda b, h: (b, 0, off_q // dk + h)),
                pl.BlockSpec((bb, T, dk), lambda b, h: (b, 0, off_k // dk + h)),
                pl.BlockSpec((bb, T, dv), lambda b, h: (b, 0, off_v // dv + h)),
                pl.BlockSpec((bb, T, dv), lambda b, h: (b, 0, off_g // dv + h))]
    args = [lg, pa3, pa3, pa3, pa3]
    blocks = bb * (2 * _nbytes((T, dk), BF16) + 3 * _nbytes((T, dv), BF16)) + _nbytes((8, V7X_LANES), F32)
    state_spec = pl.BlockSpec((bb, None, None, dk, dv), lambda b, h: (b, 0, h, 0, 0))
    if has_state:
        in_specs += [state_spec, state_spec]
        args += list(states)
        blocks += 2 * bb * _nbytes((dk, dv), F32)
    out_shape = [jax.ShapeDtypeStruct((B, T, H * dv), BF16)]
    out_specs = [pl.BlockSpec((bb, T, dv), lambda b, h: (b, 0, h))]
    if emit_state:
        out_shape += [jax.ShapeDtypeStruct((B, 1, H, dk, dv), F32)] * 2
        out_specs += [state_spec, state_spec]
        blocks += 2 * bb * _nbytes((dk, dv), F32)
    scratch_bytes = _nbytes((bb, n, 2 * dk, dv), BF16)
    temps = 8 * _nbytes((C, dv), F32) + 8 * _nbytes((dk, dv), F32) + 4 * _nbytes((C, C), F32)
    return pl.pallas_call(
        functools.partial(_ret_kernel, has_state=has_state, emit_state=emit_state, k_scale=float(dk) ** -0.5),
        out_shape=out_shape,
        grid=(B // bb, H),
        in_specs=in_specs,
        out_specs=out_specs,
        scratch_shapes=[pltpu.VMEM((bb, n, 2 * dk, dv), BF16)],
        compiler_params=pltpu.CompilerParams(
            dimension_semantics=("parallel", "parallel"),
            vmem_limit_bytes=_vmem_limit(blocks, scratch_bytes, temps)),
        name="retention",
    )(*args)


DIFF_SUB_ROWS = 256


def _diff_kernel(lam_ref, gs_ref, q_ref, k_ref, v_ref, *rest, has_cache, lam_init, dh, sub):
    if has_cache:
        ck_ref, cv_ref, o_ref, kall, vall = rest
    else:
        o_ref, kall, vall = rest
    T = k_ref.shape[0]
    w = 2 * dh
    heads = q_ref.shape[1] // w

    kall[0:T, :] = k_ref[...].astype(kall.dtype)
    vall[0:T, :] = v_ref[...].astype(vall.dtype)
    if has_cache:
        kall[T:, :] = _pair_rotary_lanes(ck_ref[...]).astype(kall.dtype)
        vall[T:, :] = cv_ref[...].astype(vall.dtype)

    lp = lam_ref[...]
    lam = (jnp.exp(jnp.sum(lp[0:1, :] * lp[1:2, :], axis=-1, keepdims=True))
           - jnp.exp(jnp.sum(lp[2:3, :] * lp[3:4, :], axis=-1, keepdims=True)) + lam_init)
    gain = gs_ref[...] * (1.0 - lam_init)
    items = [(hh, slice(r0, r0 + sub)) for hh in range(heads) for r0 in range(0, T, sub)]

    def scores(item):
        hh, r = item
        out = []
        for m in range(2):
            cs = slice(hh * w + m * dh, hh * w + (m + 1) * dh)
            out.append(lax.dot_general(q_ref[r, cs], kall[:, cs],
                                       (((1,), (1,)), ((), ())), preferred_element_type=F32))
        return out

    s_next = scores(items[0])
    for t, (hh, r) in enumerate(items):
        s_cur = s_next
        if t + 1 < len(items):
            s_next = scores(items[t + 1])
        probs = []
        for s in s_cur:
            e = jnp.exp2(s - jnp.max(s, axis=-1, keepdims=True))
            probs.append((e, jnp.sum(e, axis=-1, keepdims=True)))
        (e1, l1), (e2, l2) = probs
        r1 = 1.0 / l1
        a = (e1 - e2 * (lam * l1 / l2)).astype(BF16)
        o = jnp.dot(a, vall[:, hh * w:(hh + 1) * w], preferred_element_type=F32) * r1
        on = o * lax.rsqrt(jnp.mean(o * o, axis=-1, keepdims=True) + SUBLN_EPS)
        o_ref[r, hh * w:(hh + 1) * w] = (on * gain).astype(o_ref.dtype)


def _diffattn(q_src, k_src, v_src, lam_params, g_subln, caches, *, lam_init, dh, heads_per_step):
    (qa, off_q), (ka, off_k), (va, off_v) = q_src, k_src, v_src
    B, T, _ = qa.shape
    H = N_DIFF_HEADS
    w = 2 * dh
    gw = heads_per_step * w
    sub = min(DIFF_SUB_ROWS, T)
    has_cache = caches is not None
    Tk = T + (caches[0].shape[1] if has_cache else 0)
    in_specs = [pl.BlockSpec((8, dh), lambda b, g: (0, 0)),
                pl.BlockSpec((1, w), lambda b, g: (0, 0)),
                pl.BlockSpec((None, T, gw), lambda b, g: (b, 0, off_q // gw + g)),
                pl.BlockSpec((None, T, gw), lambda b, g: (b, 0, off_k // gw + g)),
                pl.BlockSpec((None, T, gw), lambda b, g: (b, 0, off_v // gw + g))]
    args = [lam_params, g_subln, qa, ka, va]
    blocks = 2 * _nbytes((T, gw), BF16) + _nbytes((T, gw), ka.dtype) + _nbytes((T, gw), va.dtype)
    if has_cache:
        P = caches[0].shape[1]
        cspec = pl.BlockSpec((None, P, gw), lambda b, g: (b, 0, g))
        in_specs += [cspec, cspec]
        args += list(caches)
        blocks += 2 * _nbytes((P, gw), F32)
    temps = 10 * _nbytes((sub, Tk), F32)
    scratch_rows = [Tk, Tk]
    return pl.pallas_call(
        functools.partial(_diff_kernel, has_cache=has_cache, lam_init=lam_init, dh=dh, sub=sub),
        out_shape=jax.ShapeDtypeStruct((B, T, H * w), BF16),
        grid=(B, H // heads_per_step),
        in_specs=in_specs,
        out_specs=pl.BlockSpec((None, T, gw), lambda b, g: (b, 0, g)),
        scratch_shapes=[pltpu.VMEM((n, gw), BF16) for n in scratch_rows],
        compiler_params=pltpu.CompilerParams(
            dimension_semantics=("parallel", "parallel"),
            vmem_limit_bytes=_vmem_limit(blocks, sum(_nbytes((n, gw), BF16) for n in scratch_rows), temps)),
        name="diffattn",
    )(*args)


MERGE_A_SUB_ROWS = 256


def _merge_a_kernel(r_ref, d_ref, wr_ref, wd_ref, gr_ref, gd_ref, *rest):
    n_cast = len(rest) // 2
    cast_in, o_ref, cast_out = rest[:n_cast], rest[n_cast], rest[n_cast + 1:]
    for src, dst in zip(cast_in, cast_out):
        dst[...] = src[...].astype(dst.dtype)
    sub = min(MERGE_A_SUB_ROWS, o_ref.shape[0])
    n_sub = o_ref.shape[0] // sub

    def branch_dots(k):
        rs = slice(k * sub, (k + 1) * sub)
        return (jnp.dot(r_ref[rs, :], wr_ref[...], preferred_element_type=F32),
                jnp.dot(d_ref[rs, :], wd_ref[...], preferred_element_type=F32))

    nxt = branch_dots(0)
    for k in range(n_sub):
        rb, db = nxt
        if k + 1 < n_sub:
            nxt = branch_dots(k + 1)
        rs = slice(k * sub, (k + 1) * sub)
        gr = jax.nn.sigmoid(gr_ref[rs, :].astype(F32))
        gd = jax.nn.sigmoid(gd_ref[rs, :].astype(F32))
        o_ref[rs, :] = (gr * rb + gd * db).astype(o_ref.dtype)


def _merge_a(ret_g, diff_n, w_ret_o, w_diff_o, gates, *, off_gr, off_gd, cast_params=(), tm=1024, tn=512):
    M, Kr = ret_g.shape
    Kd = diff_n.shape[1]
    D = w_ret_o.shape[1]
    tm = min(tm, M)
    n_j = D // tn
    blocks = (_nbytes((tm, Kr), BF16) + _nbytes((tm, Kd), BF16) + _nbytes((Kr, tn), BF16)
              + _nbytes((Kd, tn), BF16) + 3 * _nbytes((tm, tn), BF16))
    c_in, c_out, c_shapes, c_bytes = _slab_cast_specs(cast_params, (M // tm) * n_j, lambda i, j: i * n_j + j)
    outs = pl.pallas_call(
        _merge_a_kernel,
        out_shape=[jax.ShapeDtypeStruct((M, D), BF16)] + c_shapes,
        grid=(M // tm, n_j),
        in_specs=[pl.BlockSpec((tm, Kr), lambda i, j: (i, 0)),
                  pl.BlockSpec((tm, Kd), lambda i, j: (i, 0)),
                  pl.BlockSpec((Kr, tn), lambda i, j: (0, j)),
                  pl.BlockSpec((Kd, tn), lambda i, j: (0, j)),
                  pl.BlockSpec((tm, tn), lambda i, j: (i, off_gr // tn + j)),
                  pl.BlockSpec((tm, tn), lambda i, j: (i, off_gd // tn + j))] + c_in,
        out_specs=[pl.BlockSpec((tm, tn), lambda i, j: (i, j))] + c_out,
        compiler_params=pltpu.CompilerParams(
            dimension_semantics=("parallel", "arbitrary"),
            vmem_limit_bytes=_vmem_limit(blocks + c_bytes, temp_bytes=4 * _nbytes((tm, tn), F32))),
        name="merge_a",
    )(ret_g, diff_n, w_ret_o, w_diff_o, gates, gates, *cast_params)
    return outs if cast_params else outs[0]


MERGE_SUB_ROWS = 256


def _merge_b_kernel(m_ref, w_ref, x_ref, mod_ref, gpost_ref, gpre_ref, x1_ref, h2_ref):
    gate_post = mod_ref[2:3, :] * gpost_ref[...]
    a2 = gpre_ref[...] * (1.0 + mod_ref[4:5, :])
    shift2 = mod_ref[3:4, :]
    sub = MERGE_SUB_ROWS
    n_sub = x_ref.shape[0] // sub

    def out_proj(k):
        return jnp.dot(m_ref[k * sub:(k + 1) * sub, :], w_ref[...], preferred_element_type=F32)

    y_next = out_proj(0)
    for k in range(n_sub):
        y_all = y_next
        if k + 1 < n_sub:
            y_next = out_proj(k + 1)
        for c in range(sub // ROW_CHUNK):
            sl = slice(k * sub + c * ROW_CHUNK, k * sub + (c + 1) * ROW_CHUNK)
            y = y_all[c * ROW_CHUNK:(c + 1) * ROW_CHUNK, :]
            x1 = x_ref[sl, :] + y * lax.rsqrt(jnp.mean(y * y, axis=-1, keepdims=True) + EPS) * gate_post
            x1_ref[sl, :] = x1
            h2_ref[sl, :] = (x1 * lax.rsqrt(jnp.mean(x1 * x1, axis=-1, keepdims=True) + EPS) * a2
                             + shift2).astype(h2_ref.dtype)


def _merge_b(merged, w_out, x2, mod, g_post, g_pre, *, rows_per_group, tm=512):
    M, D = x2.shape
    tm = min(tm, M)
    blocks = (_nbytes((tm, D), BF16) * 2 + _nbytes((D, D), BF16) + 2 * _nbytes((tm, D), F32)
              + _nbytes((N_MOD + 2, D), F32))
    return pl.pallas_call(
        _merge_b_kernel,
        out_shape=[jax.ShapeDtypeStruct((M, D), F32), jax.ShapeDtypeStruct((M, D), BF16)],
        grid=(M // tm,),
        in_specs=[pl.BlockSpec((tm, D), lambda i: (i, 0)),
                  pl.BlockSpec((D, D), lambda i: (0, 0)),
                  pl.BlockSpec((tm, D), lambda i: (i, 0)),
                  pl.BlockSpec((None, N_MOD, D), _mod_map(mod.shape[0], rows_per_group, tm)),
                  pl.BlockSpec((1, D), lambda i: (0, 0)),
                  pl.BlockSpec((1, D), lambda i: (0, 0))],
        out_specs=[pl.BlockSpec((tm, D), lambda i: (i, 0)),
                   pl.BlockSpec((tm, D), lambda i: (i, 0))],
        compiler_params=pltpu.CompilerParams(
            dimension_semantics=("parallel",),
            vmem_limit_bytes=_vmem_limit(blocks, temp_bytes=2 * _nbytes((tm, D), F32))),
        name="merge_b",
    )(merged, w_out, x2, mod, g_post, g_pre)


def _mlp_kernel(h_ref, wu_ref, wd_ref, x_ref, mod_ref, g_ref, o_ref):
    j = pl.program_id(1)
    last = pl.num_programs(1) - 1

    def part(rows):
        u = jnp.maximum(jnp.dot(h_ref[rows, :], wu_ref[...], preferred_element_type=F32), 0.0)
        return jnp.dot((u * u).astype(BF16), wd_ref[...], preferred_element_type=F32)

    @pl.when(j == 0)
    def _():
        o_ref[...] = part(slice(None))

    @pl.when(jnp.logical_and(j > 0, j < last))
    def _():
        o_ref[...] += part(slice(None))

    @pl.when(j == last)
    def _():
        gate_post = mod_ref[5:6, :] * g_ref[...]
        sub = MERGE_SUB_ROWS
        n_sub = x_ref.shape[0] // sub
        p_next = part(slice(0, sub))
        for k in range(n_sub):
            p_cur = p_next
            if k + 1 < n_sub:
                p_next = part(slice((k + 1) * sub, (k + 2) * sub))
            for c in range(sub // ROW_CHUNK):
                sl = slice(k * sub + c * ROW_CHUNK, k * sub + (c + 1) * ROW_CHUNK)
                y = o_ref[sl, :] + p_cur[c * ROW_CHUNK:(c + 1) * ROW_CHUNK, :]
                o_ref[sl, :] = (x_ref[sl, :]
                                + y * lax.rsqrt(jnp.mean(y * y, axis=-1, keepdims=True) + EPS) * gate_post)


def _mlp(h2, w_up, w_down, x1, mod, g_post, *, rows_per_group, tm=512, tf=1024):
    M, D = x1.shape
    Fh = w_up.shape[1]
    tm = min(tm, M)
    assert Fh // tf >= 2
    blocks = (_nbytes((tm, D), BF16) + 2 * _nbytes((D, tf), BF16) + 2 * _nbytes((tm, D), F32)
              + _nbytes((N_MOD + 1, D), F32))
    return pl.pallas_call(
        _mlp_kernel,
        out_shape=jax.ShapeDtypeStruct((M, D), F32),
        grid=(M // tm, Fh // tf),
        in_specs=[pl.BlockSpec((tm, D), lambda i, j: (i, 0)),
                  pl.BlockSpec((D, tf), lambda i, j: (0, j)),
                  pl.BlockSpec((tf, D), lambda i, j: (j, 0)),
                  pl.BlockSpec((tm, D), lambda i, j: (i, 0)),
                  pl.BlockSpec((None, N_MOD, D), _mod_map(mod.shape[0], rows_per_group, tm)),
                  pl.BlockSpec((1, D), lambda i, j: (0, 0))],
        out_specs=pl.BlockSpec((tm, D), lambda i, j: (i, 0)),
        compiler_params=pltpu.CompilerParams(
            dimension_semantics=("parallel", "arbitrary"),
            vmem_limit_bytes=_vmem_limit(blocks, temp_bytes=2 * _nbytes((tm, tf), F32))),
        name="mlp",
    )(h2, w_up, w_down, x1, mod, g_post)


def _rope_tables(T, rope_half):
    rows = T // GRID_W
    row = jnp.repeat(jnp.arange(rows, dtype=F32), GRID_W)
    col = jnp.tile(jnp.arange(GRID_W, dtype=F32), rows)
    inv = ROPE_BASE ** (-jnp.arange(0, rope_half, 2, dtype=F32) / rope_half)
    ar, ac = row[:, None] * inv, col[:, None] * inv
    cos = jnp.concatenate([jnp.cos(ar), jnp.cos(ac), jnp.cos(ar), jnp.cos(ac)], axis=-1)
    sin = jnp.concatenate([-jnp.sin(ar), -jnp.sin(ac), jnp.sin(ar), jnp.sin(ac)], axis=-1)
    return cos, sin


def kernel(x_prompt, x_sample, cache_k, cache_v, state_ret_fwd, state_ret_bwd, c, c_ctx, w_ada, b_ada, g_mix_pre, g_mix_post, g_mlp_pre, g_mlp_post, w_in, ret_gamma_logit_fwd, ret_gamma_logit_bwd, w_ret_o, lambda_q1, lambda_k1, lambda_q2, lambda_k2, g_diff_subln, w_diff_o, w_out, w_mlp_up, w_mlp_down):
    Bp, Tp, D = x_prompt.shape
    Bs, Ts, _ = x_sample.shape
    depth = w_in.shape[0]
    assert depth == 1
    l = 0
    H = N_RET_HEADS
    dk = D // H
    dv = 2 * dk
    dh = D // N_DIFF_HEADS // 2
    ret_qk_w, ret_v_w, diff_w = H * dk, H * dv, N_DIFF_HEADS * 2 * dh
    o_dq = 2 * ret_qk_w + 2 * ret_v_w
    o_dk, o_dv, o_gate = o_dq + diff_w, o_dq + 2 * diff_w, o_dq + 3 * diff_w
    in_w = o_gate + 2 * D
    assert w_in.shape[2] == in_w
    lam_init = 0.8 - 0.6 * math.exp(-0.3 * l)
    q_mult = float(dh) ** -0.5 * math.log2(math.e)

    w_in_l = w_in[l]
    g_mix_pre_l = g_mix_pre[l].reshape(1, D)
    g_mix_post_l = g_mix_post[l].reshape(1, D)
    g_mlp_pre_l = g_mlp_pre[l].reshape(1, D)
    g_mlp_post_l = g_mlp_post[l].reshape(1, D)
    g_subln_l = g_diff_subln[l].reshape(1, 2 * dh)
    lam_params = jnp.concatenate(
        [jnp.stack([lambda_q1[l], lambda_k1[l], lambda_q2[l], lambda_k2[l]]), jnp.zeros((4, dh), F32)], axis=0)
    lg = jnp.stack([ret_gamma_logit_fwd[l], ret_gamma_logit_bwd[l]], axis=1)
    lg = jnp.concatenate([lg, jnp.zeros((H, 6), F32)], axis=1)
    lg = jnp.broadcast_to(lg[:, :, None], (H, 8, V7X_LANES))

    n_cond = 16
    cond = jnp.concatenate([c, jnp.broadcast_to(c_ctx[None, :], (n_cond - Bs, D))], axis=0)
    mod = _adaln(cond, w_ada[l], b_ada[l].reshape(1, N_MOD * D)).reshape(n_cond, N_MOD, D)

    ret_offs = dict(off_q=0, off_k=ret_qk_w, off_v=2 * ret_qk_w, off_g=2 * ret_qk_w + ret_v_w)

    xs2 = x_sample.reshape(Bs * Ts, D)
    hs = _prenorm(xs2, mod, g_mix_pre_l, rows_per_group=Ts)
    sa = _proj(hs, w_in_l, (0, o_dq), out_dtype=BF16, tm=2048, name="proj_ret").reshape(Bs, Ts, -1)
    sqk, w_ret_o_b, w_diff_o_b, w_out_b = _proj(
        hs, w_in_l, (o_dq, o_dv), out_dtype=BF16, n_q_cols=diff_w, q_mult=q_mult,
        rope_tables=_rope_tables(Ts, dh // 2), cast_params=(w_ret_o[l], w_diff_o[l], w_out[l]),
        name="proj_qk_rope")
    sqk = sqk.reshape(Bs, Ts, 2 * diff_w)
    svg = _proj(hs, w_in_l, (o_dv, in_w), out_dtype=BF16, tm=2048, name="proj_vg")
    (ret_g,) = _retention(sa, lg, (state_ret_fwd, state_ret_bwd), emit_state=False, dk=dk, dv=dv, **ret_offs)
    caches = (cache_k[:, l].reshape(Bs, -1, diff_w), cache_v[:, l].reshape(Bs, -1, diff_w))
    diff_n = _diffattn((sqk, 0), (sqk, diff_w), (svg.reshape(Bs, Ts, -1), 0),
                       lam_params, g_subln_l, caches, lam_init=lam_init, dh=dh, heads_per_step=1)
    merged, w_up_b, w_down_b = _merge_a(
        ret_g.reshape(Bs * Ts, ret_v_w), diff_n.reshape(Bs * Ts, diff_w), w_ret_o_b, w_diff_o_b, svg,
        off_gr=diff_w, off_gd=diff_w + D, cast_params=(w_mlp_up[l], w_mlp_down[l]))

    def tail(merged, x2, B, T, rows_per_group):
        x1, h2 = _merge_b(merged, w_out_b, x2, mod, g_mix_post_l, g_mlp_pre_l, rows_per_group=rows_per_group)
        y = _mlp(h2, w_up_b, w_down_b, x1, mod, g_mlp_post_l, rows_per_group=rows_per_group)
        return y.reshape(B, T, D)

    y_sample = tail(merged, xs2, Bs, Ts, Ts)

    xp2 = x_prompt.reshape(Bp * Tp, D)
    hp = _prenorm(xp2, mod, g_mix_pre_l, rows_per_group=None)
    pa = _proj(hp, w_in_l, (0, o_dq), out_dtype=BF16, tm=2048, name="proj_ret").reshape(Bp, Tp, -1)
    pq = _proj(hp, w_in_l, (o_dq, o_dk), out_dtype=BF16, n_q_cols=diff_w, q_mult=q_mult,
               name="proj_q").reshape(Bp, Tp, diff_w)
    pk = _proj(hp, w_in_l, (o_dk, o_dv), out_dtype=F32, name="proj_k")
    pv = _proj(hp, w_in_l, (o_dv, o_gate), out_dtype=F32, name="proj_v")
    pg = _proj(hp, w_in_l, (o_gate, in_w), out_dtype=BF16, tm=2048, name="proj_gates")
    ret_g, new_state_fwd, new_state_bwd = _retention(pa, lg, None, emit_state=True, dk=dk, dv=dv,
                                                     batch_block=4, **ret_offs)
    diff_n = _diffattn((pq, 0), (pk.reshape(Bp, Tp, diff_w), 0), (pv.reshape(Bp, Tp, diff_w), 0),
                       lam_params, g_subln_l, None, lam_init=lam_init, dh=dh, heads_per_step=N_DIFF_HEADS)
    merged = _merge_a(ret_g.reshape(Bp * Tp, ret_v_w), diff_n.reshape(Bp * Tp, diff_w), w_ret_o_b, w_diff_o_b, pg,
                      off_gr=0, off_gd=D)
    y_prompt = tail(merged, xp2, Bp, Tp, None)
    new_cache_k = pk.reshape(Bp, 1, Tp, N_DIFF_HEADS, 2, dh)
    new_cache_v = pv.reshape(Bp, 1, Tp, N_DIFF_HEADS, 2 * dh)

    return (y_prompt, y_sample, new_cache_k, new_cache_v, new_state_fwd, new_state_bwd)
```

```python
import functools
import math

import jax
import jax.numpy as jnp
from jax import lax
from jax.experimental import pallas as pl
from jax.experimental.pallas import tpu as pltpu

F32 = jnp.float32
BF16 = jnp.bfloat16

N_RET_HEADS = 8
N_DIFF_HEADS = 8
N_MOD = 6
GRID_W = 64
ROPE_BASE = 10000.0
EPS = 1e-6
SUBLN_EPS = 1e-5

V7X_LANES = 128
V7X_VMEM_BYTES = 64 * 1024 * 1024
VMEM_RESERVE_BYTES = 8 * 1024 * 1024


def _vmem_limit(block_bytes, scratch_bytes=0, temp_bytes=0):
    want = 2 * block_bytes + scratch_bytes + temp_bytes + VMEM_RESERVE_BYTES
    return int(min(want, V7X_VMEM_BYTES - VMEM_RESERVE_BYTES // 2))


def _nbytes(shape, dtype):
    return math.prod(shape) * jnp.dtype(dtype).itemsize


def _mod_map(n_groups, rows_per_group, tm):
    if rows_per_group is None:
        return lambda i, *_: (n_groups - 1, 0, 0)
    return lambda i, *_: ((i * tm) // rows_per_group, 0, 0)


def _adaln_kernel(c_ref, w_ref, b_ref, o_ref):
    c = c_ref[...]
    s = (c * jax.nn.sigmoid(c)).astype(BF16)
    o_ref[...] = jnp.dot(s, w_ref[...].astype(BF16), preferred_element_type=F32) + b_ref[...]


def _adaln(cond, w_ada, b_ada, tn=512):
    R, D = cond.shape
    N = w_ada.shape[1]
    blocks = _nbytes((R, D), F32) + _nbytes((D, tn), F32) + _nbytes((1, tn), F32) + _nbytes((R, tn), F32)
    return pl.pallas_call(
        _adaln_kernel,
        out_shape=jax.ShapeDtypeStruct((R, N), F32),
        grid=(N // tn,),
        in_specs=[pl.BlockSpec((R, D), lambda j: (0, 0)),
                  pl.BlockSpec((D, tn), lambda j: (0, j)),
                  pl.BlockSpec((1, tn), lambda j: (0, j))],
        out_specs=pl.BlockSpec((R, tn), lambda j: (0, j)),
        compiler_params=pltpu.CompilerParams(
            dimension_semantics=("arbitrary",),
            vmem_limit_bytes=_vmem_limit(blocks, temp_bytes=_nbytes((D, tn), BF16))),
        name="adaln",
    )(cond, w_ada, b_ada)


ROW_CHUNK = 64
NORM_UNROLL = 4


def _norm_modulate(x_ref, gain, shift, scale, h_ref):
    a = gain * (1.0 + scale)
    n = x_ref.shape[0] // ROW_CHUNK

    def body(r, carry):
        sl = pl.ds(pl.multiple_of(r * ROW_CHUNK, ROW_CHUNK), ROW_CHUNK)
        x = x_ref[sl, :]
        ms = jnp.mean(x * x, axis=-1, keepdims=True)
        h_ref[sl, :] = (x * lax.rsqrt(ms + EPS) * a + shift).astype(h_ref.dtype)
        return carry

    lax.fori_loop(0, n, body, 0, unroll=NORM_UNROLL)


def _prenorm_kernel(x_ref, mod_ref, g_ref, h_ref):
    _norm_modulate(x_ref, g_ref[...], mod_ref[0:1, :], mod_ref[1:2, :], h_ref)


def _prenorm(x2, mod, gain, *, rows_per_group, tm=1024):
    M, D = x2.shape
    blocks = _nbytes((tm, D), F32) + _nbytes((tm, D), BF16) + _nbytes((N_MOD + 1, D), F32)
    return pl.pallas_call(
        _prenorm_kernel,
        out_shape=jax.ShapeDtypeStruct((M, D), BF16),
        grid=(M // tm,),
        in_specs=[pl.BlockSpec((tm, D), lambda i: (i, 0)),
                  pl.BlockSpec((None, N_MOD, D), _mod_map(mod.shape[0], rows_per_group, tm)),
                  pl.BlockSpec((1, D), lambda i: (0, 0))],
        out_specs=pl.BlockSpec((tm, D), lambda i: (i, 0)),
        compiler_params=pltpu.CompilerParams(
            dimension_semantics=("parallel",),
            vmem_limit_bytes=_vmem_limit(blocks, temp_bytes=4 * _nbytes((ROW_CHUNK, D), F32))),
        name="prenorm",
    )(x2, mod, gain)


CAST_ROWS = 256
ROPE_GROUP = V7X_LANES // 4


def _pair_rotary_lanes(x):
    n = x.shape[-1]
    group = (lax.broadcasted_iota(jnp.int32, x.shape, x.ndim - 1) % V7X_LANES) // ROPE_GROUP
    from_next = pltpu.roll(x, n - ROPE_GROUP, x.ndim - 1)
    from_prev = pltpu.roll(x, ROPE_GROUP, x.ndim - 1)
    return jnp.where(group == 1, from_next, jnp.where(group == 2, from_prev, x))


def _slab_cast_specs(params, n_steps, step_index):
    in_specs, out_specs, out_shapes, nbytes = [], [], [], 0
    for p in params:
        rows, width = p.shape[0] // n_steps, p.shape[1]
        assert rows * n_steps == p.shape[0] and rows % 16 == 0
        spec = pl.BlockSpec((rows, width), lambda *ids: (step_index(*ids), 0))
        in_specs.append(spec)
        out_specs.append(spec)
        out_shapes.append(jax.ShapeDtypeStruct(p.shape, BF16))
        nbytes += _nbytes((rows, width), F32) + _nbytes((rows, width), BF16)
    return in_specs, out_specs, out_shapes, nbytes


def _proj_kernel(h_ref, w_ref, *rest, n_q_blocks, q_mult, rope, n_cast):
    rest = list(rest)
    if rope:
        cos_ref, sin_ref = rest[:2]
        rest = rest[2:]
    cast_in, o_ref, cast_out, wb_ref = rest[:n_cast], rest[n_cast], rest[n_cast + 1:2 * n_cast + 1], rest[-1]
    for src, dst in zip(cast_in, cast_out):
        dst[...] = src[...].astype(dst.dtype)

    @pl.when(pl.program_id(1) == 0)
    def _():
        def body(r, carry):
            sl = pl.ds(pl.multiple_of(r * CAST_ROWS, CAST_ROWS), CAST_ROWS)
            wf = w_ref[sl, :]
            wb_ref[sl, :] = (_pair_rotary_lanes(wf) if rope else wf).astype(wb_ref.dtype)
            return carry

        lax.fori_loop(0, w_ref.shape[0] // CAST_ROWS, body, 0)

    mult = None
    if n_q_blocks:
        mult = jnp.where(pl.program_id(0) < n_q_blocks, q_mult, 1.0).astype(F32)
    acc = jnp.dot(h_ref[...], wb_ref[...], preferred_element_type=F32)
    if rope:
        cos, sin = cos_ref[...] * mult, sin_ref[...] * mult
        for c in range(acc.shape[1] // V7X_LANES):
            cs = slice(c * V7X_LANES, (c + 1) * V7X_LANES)
            xc = acc[:, cs]
            o_ref[:, cs] = (xc * cos + pltpu.roll(xc, V7X_LANES // 2, 1) * sin).astype(o_ref.dtype)
    elif mult is not None:
        o_ref[...] = (acc * mult).astype(o_ref.dtype)
    else:
        o_ref[...] = acc.astype(o_ref.dtype)


def _proj(h, w, cols, *, out_dtype, n_q_cols=0, q_mult=1.0, rope_tables=None, cast_params=(),
          tm=1024, tn=1024, name="proj"):
    M, D = h.shape
    N = cols[1] - cols[0]
    tm = min(tm, M)
    tn = min(tn, N)
    j0 = cols[0] // tn
    in_specs = [pl.BlockSpec((tm, D), lambda j, i: (i, 0)),
                pl.BlockSpec((D, tn), lambda j, i: (0, j0 + j))]
    args = [h, w]
    blocks = _nbytes((tm, D), BF16) + _nbytes((D, tn), w.dtype) + _nbytes((tm, tn), out_dtype)
    if rope_tables is not None:
        nt = rope_tables[0].shape[0] // tm
        for t in rope_tables:
            in_specs.append(pl.BlockSpec((tm, V7X_LANES), lambda j, i: (i % nt, 0)))
            args.append(t)
        blocks += len(rope_tables) * _nbytes((tm, V7X_LANES), F32)
    n_i = M // tm
    c_in, c_out, c_shapes, c_bytes = _slab_cast_specs(cast_params, (N // tn) * n_i, lambda j, i: j * n_i + i)
    outs = pl.pallas_call(
        functools.partial(_proj_kernel, n_q_blocks=n_q_cols // tn, q_mult=q_mult, rope=rope_tables is not None,
                          n_cast=len(cast_params)),
        out_shape=[jax.ShapeDtypeStruct((M, N), out_dtype)] + c_shapes,
        grid=(N // tn, n_i),
        in_specs=in_specs + c_in,
        out_specs=[pl.BlockSpec((tm, tn), lambda j, i: (i, j))] + c_out,
        scratch_shapes=[pltpu.VMEM((D, tn), BF16)],
        compiler_params=pltpu.CompilerParams(
            dimension_semantics=("parallel", "arbitrary"),
            vmem_limit_bytes=_vmem_limit(blocks + c_bytes, _nbytes((D, tn), BF16), _nbytes((tm, tn), F32))),
        name=name,
    )(*args, *cast_params)
    return outs if cast_params else outs[0]


RET_CHUNK = 256


def _log_sigmoid(x):
    return jnp.minimum(x, 0.0) - jnp.log1p(jnp.exp(-jnp.abs(x)))


def _ret_kernel(lg_ref, q_ref, k_ref, v_ref, g_ref, *rest, has_state, emit_state, k_scale):
    rest = list(rest)
    s0f_ref = s0b_ref = sf_out = sb_out = None
    if has_state:
        s0f_ref, s0b_ref = rest[:2]
        rest = rest[2:]
    o_ref = rest[0]
    rest = rest[1:]
    if emit_state:
        sf_out, sb_out = rest[:2]
        rest = rest[2:]
    (st_scr,) = rest

    bb, T, dk = q_ref.shape
    C = min(RET_CHUNK, T)
    n = T // C

    lsig = _log_sigmoid(lg_ref[...])
    lgf = lsig[0:1, 0:1]
    lgb = lsig[1:2, 0:1]
    ri = lax.broadcasted_iota(jnp.int32, (C, C), 0)
    ci = lax.broadcasted_iota(jnp.int32, (C, C), 1)
    rel = (ri - ci).astype(F32)
    decay = jnp.where(rel > 0, jnp.exp(rel * lgf), jnp.where(rel < 0, jnp.exp(-rel * lgb), 2.0)) * k_scale
    li = lax.broadcasted_iota(jnp.int32, (C, dk), 0).astype(F32)
    xi_f = jnp.exp((li + 1.0) * lgf)
    xi_b = jnp.exp((C - li) * lgb)
    zeta_f = jnp.exp((C - 1.0 - li) * lgf) * k_scale
    zeta_b = jnp.exp(li * lgb) * k_scale
    gc_f = jnp.exp(C * lgf)
    gc_b = jnp.exp(C * lgb)

    def rows(c):
        return slice(c * C, (c + 1) * C)

    def fwd_state_used(c):
        return has_state or c > 0

    def bwd_state_used(c):
        return has_state or c < n - 1

    def kv_pair(item):
        b, t = item
        out = []
        for c, zeta in ((t, zeta_f), (n - 1 - t, zeta_b)):
            ks = (k_ref[b, rows(c), :].astype(F32) * zeta).astype(BF16)
            out.append(lax.dot_general(ks, v_ref[b, rows(c), :], (((0,), (0,)), ((), ())),
                                       preferred_element_type=F32))
        return out

    n_upd = n if emit_state else n - 1
    items = [(b, t) for b in range(bb) for t in range(n_upd)]
    kv_next = kv_pair(items[0]) if items else None
    pos = 0
    for b in range(bb):
        sf = s0f_ref[b] if has_state else None
        sb = s0b_ref[b] if has_state else None
        for t in range(n):
            cf, cb = t, n - 1 - t
            if fwd_state_used(cf):
                st_scr[b, cf, 0:dk, :] = sf.astype(BF16)
            if bwd_state_used(cb):
                st_scr[b, cb, dk:2 * dk, :] = sb.astype(BF16)
            if t < n_upd:
                kvf, kvb = kv_next
                pos += 1
                if pos < len(items):
                    kv_next = kv_pair(items[pos])
                sf = kvf if sf is None else gc_f * sf + kvf
                sb = kvb if sb is None else gc_b * sb + kvb
        if emit_state:
            sf_out[b] = sf
            sb_out[b] = sb

    def chunk_out(item):
        b, c = item
        qc = q_ref[b, rows(c), :]
        s = lax.dot_general(qc, k_ref[b, rows(c), :], (((1,), (1,)), ((), ())),
                            preferred_element_type=F32) * decay
        o = jnp.dot(s.astype(BF16), v_ref[b, rows(c), :], preferred_element_type=F32)
        qf = qc.astype(F32)
        parts = []
        if fwd_state_used(c):
            parts.append(((qf * xi_f).astype(BF16), st_scr[b, c, 0:dk, :]))
        if bwd_state_used(c):
            parts.append(((qf * xi_b).astype(BF16), st_scr[b, c, dk:2 * dk, :]))
        if len(parts) == 2:
            o = o + jnp.dot(jnp.concatenate([parts[0][0], parts[1][0]], axis=1), st_scr[b, c],
                            preferred_element_type=F32)
        elif parts:
            o = o + jnp.dot(parts[0][0], parts[0][1], preferred_element_type=F32)
        return o

    items = [(b, c) for b in range(bb) for c in range(n)]
    o_next = chunk_out(items[0])
    for pos, (b, c) in enumerate(items):
        o = o_next
        if pos + 1 < len(items):
            o_next = chunk_out(items[pos + 1])
        on = o * lax.rsqrt(jnp.mean(o * o, axis=-1, keepdims=True) + EPS)
        hg = g_ref[b, rows(c), :] * 0.5
        o_ref[b, rows(c), :] = (hg + hg * jnp.tanh(hg)) * on.astype(o_ref.dtype)


def _retention(pa3, lg, states, *, emit_state, dk, dv, off_q, off_k, off_v, off_g, batch_block=1):
    B, T, _ = pa3.shape
    H = N_RET_HEADS
    bb = batch_block
    has_state = states is not None
    C = min(RET_CHUNK, T)
    n = T // C
    in_specs = [pl.BlockSpec((None, 8, V7X_LANES), lambda b, h: (h, 0, 0)),
                pl.BlockSpec((bb, T, dk), lambda b, h: (b, 0, off_q // dk + h)),
                pl.BlockSpec((bb, T, dk), lambda b, h: (b, 0, off_k // dk + h)),
                pl.BlockSpec((bb, T, dv), lambda b, h: (b, 0, off_v // dv + h)),
                pl.BlockSpec((bb, T, dv), lambda b, h: (b, 0, off_g // dv + h))]
    args = [lg, pa3, pa3, pa3, pa3]
    blocks = bb * (2 * _nbytes((T, dk), BF16) + 3 * _nbytes((T, dv), BF16)) + _nbytes((8, V7X_LANES), F32)
    state_spec = pl.BlockSpec((bb, None, None, dk, dv), lambda b, h: (b, 0, h, 0, 0))
    if has_state:
        in_specs += [state_spec, state_spec]
        args += list(states)
        blocks += 2 * bb * _nbytes((dk, dv), F32)
    out_shape = [jax.ShapeDtypeStruct((B, T, H * dv), BF16)]
    out_specs = [pl.BlockSpec((bb, T, dv), lambda b, h: (b, 0, h))]
    if emit_state:
        out_shape += [jax.ShapeDtypeStruct((B, 1, H, dk, dv), F32)] * 2
        out_specs += [state_spec, state_spec]
        blocks += 2 * bb * _nbytes((dk, dv), F32)
    scratch_bytes = _nbytes((bb, n, 2 * dk, dv), BF16)
    temps = 8 * _nbytes((C, dv), F32) + 8 * _nbytes((dk, dv), F32) + 4 * _nbytes((C, C), F32)
    return pl.pallas_call(
        functools.partial(_ret_kernel, has_state=has_state, emit_state=emit_state, k_scale=float(dk) ** -0.5),
        out_shape=out_shape,
        grid=(B // bb, H),
        in_specs=in_specs,
        out_specs=out_specs,
        scratch_shapes=[pltpu.VMEM((bb, n, 2 * dk, dv), BF16)],
        compiler_params=pltpu.CompilerParams(
            dimension_semantics=("parallel", "parallel"),
            vmem_limit_bytes=_vmem_limit(blocks, scratch_bytes, temps)),
        name="retention",
    )(*args)


DIFF_SUB_ROWS = 256
KEY_MAJOR_MAX_KEYS = 512


def _diff_kernel(lam_ref, gs_ref, q_ref, k_ref, v_ref, *rest, has_cache, lam_init, dh, sub, key_major):
    if has_cache:
        ck_ref, cv_ref, o_ref, kall, vall = rest
    else:
        o_ref, kall, vall = rest
    T = k_ref.shape[0]
    w = 2 * dh
    heads = q_ref.shape[1] // w

    kall[0:T, :] = k_ref[...].astype(kall.dtype)
    vall[0:T, :] = v_ref[...].astype(vall.dtype)
    if has_cache:
        kall[T:, :] = _pair_rotary_lanes(ck_ref[...]).astype(kall.dtype)
        vall[T:, :] = cv_ref[...].astype(vall.dtype)

    lp = lam_ref[...]
    lam = (jnp.exp(jnp.sum(lp[0:1, :] * lp[1:2, :], axis=-1, keepdims=True))
           - jnp.exp(jnp.sum(lp[2:3, :] * lp[3:4, :], axis=-1, keepdims=True)) + lam_init)
    gain = gs_ref[...] * (1.0 - lam_init)
    items = [(hh, slice(r0, r0 + sub)) for hh in range(heads) for r0 in range(0, T, sub)]

    red = 0 if key_major else 1

    def scores(item):
        hh, r = item
        out = []
        for m in range(2):
            cs = slice(hh * w + m * dh, hh * w + (m + 1) * dh)
            lhs, rhs = (kall[:, cs], q_ref[r, cs]) if key_major else (q_ref[r, cs], kall[:, cs])
            out.append(lax.dot_general(lhs, rhs, (((1,), (1,)), ((), ())), preferred_element_type=F32))
        return out

    s_next = scores(items[0])
    for t, (hh, r) in enumerate(items):
        s_cur = s_next
        if t + 1 < len(items):
            s_next = scores(items[t + 1])
        probs = []
        for s in s_cur:
            e = jnp.exp2(s - jnp.max(s, axis=red, keepdims=True))
            probs.append((e, jnp.sum(e, axis=red, keepdims=True)))
        (e1, l1), (e2, l2) = probs
        v_h = vall[:, hh * w:(hh + 1) * w]
        if key_major:
            a_t = (e1 * (1.0 / l1) - e2 * (lam / l2)).astype(BF16)
            o = lax.dot_general(a_t, v_h, (((0,), (0,)), ((), ())), preferred_element_type=F32)
        else:
            a = (e1 - e2 * (lam * l1 / l2)).astype(BF16)
            o = jnp.dot(a, v_h, preferred_element_type=F32) * (1.0 / l1)
        on = o * lax.rsqrt(jnp.mean(o * o, axis=-1, keepdims=True) + SUBLN_EPS)
        o_ref[r, hh * w:(hh + 1) * w] = (on * gain).astype(o_ref.dtype)


def _diffattn(q_src, k_src, v_src, lam_params, g_subln, caches, *, lam_init, dh, heads_per_step):
    (qa, off_q), (ka, off_k), (va, off_v) = q_src, k_src, v_src
    B, T, _ = qa.shape
    H = N_DIFF_HEADS
    w = 2 * dh
    gw = heads_per_step * w
    sub = min(DIFF_SUB_ROWS, T)
    has_cache = caches is not None
    Tk = T + (caches[0].shape[1] if has_cache else 0)
    in_specs = [pl.BlockSpec((8, dh), lambda b, g: (0, 0)),
                pl.BlockSpec((1, w), lambda b, g: (0, 0)),
                pl.BlockSpec((None, T, gw), lambda b, g: (b, 0, off_q // gw + g)),
                pl.BlockSpec((None, T, gw), lambda b, g: (b, 0, off_k // gw + g)),
                pl.BlockSpec((None, T, gw), lambda b, g: (b, 0, off_v // gw + g))]
    args = [lam_params, g_subln, qa, ka, va]
    blocks = 2 * _nbytes((T, gw), BF16) + _nbytes((T, gw), ka.dtype) + _nbytes((T, gw), va.dtype)
    if has_cache:
        P = caches[0].shape[1]
        cspec = pl.BlockSpec((None, P, gw), lambda b, g: (b, 0, g))
        in_specs += [cspec, cspec]
        args += list(caches)
        blocks += 2 * _nbytes((P, gw), F32)
    temps = 10 * _nbytes((sub, Tk), F32)
    scratch_rows = [Tk, Tk]
    return pl.pallas_call(
        functools.partial(_diff_kernel, has_cache=has_cache, lam_init=lam_init, dh=dh, sub=sub,
                          key_major=Tk <= KEY_MAJOR_MAX_KEYS),
        out_shape=jax.ShapeDtypeStruct((B, T, H * w), BF16),
        grid=(B, H // heads_per_step),
        in_specs=in_specs,
        out_specs=pl.BlockSpec((None, T, gw), lambda b, g: (b, 0, g)),
        scratch_shapes=[pltpu.VMEM((n, gw), BF16) for n in scratch_rows],
        compiler_params=pltpu.CompilerParams(
            dimension_semantics=("parallel", "parallel"),
            vmem_limit_bytes=_vmem_limit(blocks, sum(_nbytes((n, gw), BF16) for n in scratch_rows), temps)),
        name="diffattn",
    )(*args)


MERGE_A_SUB_ROWS = 256


def _merge_a_kernel(r_ref, d_ref, wr_ref, wd_ref, gr_ref, gd_ref, *rest):
    n_cast = len(rest) // 2
    cast_in, o_ref, cast_out = rest[:n_cast], rest[n_cast], rest[n_cast + 1:]
    for src, dst in zip(cast_in, cast_out):
        dst[...] = src[...].astype(dst.dtype)
    sub = min(MERGE_A_SUB_ROWS, o_ref.shape[0])
    n_sub = o_ref.shape[0] // sub

    def branch_dots(k):
        rs = slice(k * sub, (k + 1) * sub)
        return (jnp.dot(r_ref[rs, :], wr_ref[...], preferred_element_type=F32),
                jnp.dot(d_ref[rs, :], wd_ref[...], preferred_element_type=F32))

    nxt = branch_dots(0)
    for k in range(n_sub):
        rb, db = nxt
        if k + 1 < n_sub:
            nxt = branch_dots(k + 1)
        rs = slice(k * sub, (k + 1) * sub)
        gr = jax.nn.sigmoid(gr_ref[rs, :].astype(F32))
        gd = jax.nn.sigmoid(gd_ref[rs, :].astype(F32))
        o_ref[rs, :] = (gr * rb + gd * db).astype(o_ref.dtype)


def _merge_a(ret_g, diff_n, w_ret_o, w_diff_o, gates, *, off_gr, off_gd, cast_params=(), tm=1024, tn=512):
    M, Kr = ret_g.shape
    Kd = diff_n.shape[1]
    D = w_ret_o.shape[1]
    tm = min(tm, M)
    n_j = D // tn
    blocks = (_nbytes((tm, Kr), BF16) + _nbytes((tm, Kd), BF16) + _nbytes((Kr, tn), BF16)
              + _nbytes((Kd, tn), BF16) + 3 * _nbytes((tm, tn), BF16))
    c_in, c_out, c_shapes, c_bytes = _slab_cast_specs(cast_params, (M // tm) * n_j, lambda i, j: i * n_j + j)
    outs = pl.pallas_call(
        _merge_a_kernel,
        out_shape=[jax.ShapeDtypeStruct((M, D), BF16)] + c_shapes,
        grid=(M // tm, n_j),
        in_specs=[pl.BlockSpec((tm, Kr), lambda i, j: (i, 0)),
                  pl.BlockSpec((tm, Kd), lambda i, j: (i, 0)),
                  pl.BlockSpec((Kr, tn), lambda i, j: (0, j)),
                  pl.BlockSpec((Kd, tn), lambda i, j: (0, j)),
                  pl.BlockSpec((tm, tn), lambda i, j: (i, off_gr // tn + j)),
                  pl.BlockSpec((tm, tn), lambda i, j: (i, off_gd // tn + j))] + c_in,
        out_specs=[pl.BlockSpec((tm, tn), lambda i, j: (i, j))] + c_out,
        compiler_params=pltpu.CompilerParams(
            dimension_semantics=("parallel", "arbitrary"),
            vmem_limit_bytes=_vmem_limit(blocks + c_bytes, temp_bytes=4 * _nbytes((tm, tn), F32))),
        name="merge_a",
    )(ret_g, diff_n, w_ret_o, w_diff_o, gates, gates, *cast_params)
    return outs if cast_params else outs[0]


MERGE_SUB_ROWS = 256


def _merge_b_kernel(m_ref, w_ref, x_ref, mod_ref, gpost_ref, gpre_ref, x1_ref, h2_ref):
    gate_post = mod_ref[2:3, :] * gpost_ref[...]
    a2 = gpre_ref[...] * (1.0 + mod_ref[4:5, :])
    shift2 = mod_ref[3:4, :]
    sub = MERGE_SUB_ROWS
    n_sub = x_ref.shape[0] // sub

    def out_proj(k):
        return jnp.dot(m_ref[k * sub:(k + 1) * sub, :], w_ref[...], preferred_element_type=F32)

    y_next = out_proj(0)
    for k in range(n_sub):
        y_all = y_next
        if k + 1 < n_sub:
            y_next = out_proj(k + 1)
        for c in range(sub // ROW_CHUNK):
            sl = slice(k * sub + c * ROW_CHUNK, k * sub + (c + 1) * ROW_CHUNK)
            y = y_all[c * ROW_CHUNK:(c + 1) * ROW_CHUNK, :]
            x1 = x_ref[sl, :] + y * lax.rsqrt(jnp.mean(y * y, axis=-1, keepdims=True) + EPS) * gate_post
            x1_ref[sl, :] = x1
            h2_ref[sl, :] = (x1 * lax.rsqrt(jnp.mean(x1 * x1, axis=-1, keepdims=True) + EPS) * a2
                             + shift2).astype(h2_ref.dtype)


def _merge_b(merged, w_out, x2, mod, g_post, g_pre, *, rows_per_group, tm=512):
    M, D = x2.shape
    tm = min(tm, M)
    blocks = (_nbytes((tm, D), BF16) * 2 + _nbytes((D, D), BF16) + 2 * _nbytes((tm, D), F32)
              + _nbytes((N_MOD + 2, D), F32))
    return pl.pallas_call(
        _merge_b_kernel,
        out_shape=[jax.ShapeDtypeStruct((M, D), F32), jax.ShapeDtypeStruct((M, D), BF16)],
        grid=(M // tm,),
        in_specs=[pl.BlockSpec((tm, D), lambda i: (i, 0)),
                  pl.BlockSpec((D, D), lambda i: (0, 0)),
                  pl.BlockSpec((tm, D), lambda i: (i, 0)),
                  pl.BlockSpec((None, N_MOD, D), _mod_map(mod.shape[0], rows_per_group, tm)),
                  pl.BlockSpec((1, D), lambda i: (0, 0)),
                  pl.BlockSpec((1, D), lambda i: (0, 0))],
        out_specs=[pl.BlockSpec((tm, D), lambda i: (i, 0)),
                   pl.BlockSpec((tm, D), lambda i: (i, 0))],
        compiler_params=pltpu.CompilerParams(
            dimension_semantics=("parallel",),
            vmem_limit_bytes=_vmem_limit(blocks, temp_bytes=2 * _nbytes((tm, D), F32))),
        name="merge_b",
    )(merged, w_out, x2, mod, g_post, g_pre)


def _mlp_kernel(h_ref, wu_ref, wd_ref, x_ref, mod_ref, g_ref, o_ref):
    j = pl.program_id(1)
    last = pl.num_programs(1) - 1

    def part(rows):
        u = jnp.maximum(jnp.dot(h_ref[rows, :], wu_ref[...], preferred_element_type=F32), 0.0)
        return jnp.dot((u * u).astype(BF16), wd_ref[...], preferred_element_type=F32)

    @pl.when(j == 0)
    def _():
        o_ref[...] = part(slice(None))

    @pl.when(jnp.logical_and(j > 0, j < last))
    def _():
        o_ref[...] += part(slice(None))

    @pl.when(j == last)
    def _():
        gate_post = mod_ref[5:6, :] * g_ref[...]
        sub = MERGE_SUB_ROWS
        n_sub = x_ref.shape[0] // sub
        p_next = part(slice(0, sub))
        for k in range(n_sub):
            p_cur = p_next
            if k + 1 < n_sub:
                p_next = part(slice((k + 1) * sub, (k + 2) * sub))
            for c in range(sub // ROW_CHUNK):
                sl = slice(k * sub + c * ROW_CHUNK, k * sub + (c + 1) * ROW_CHUNK)
                y = o_ref[sl, :] + p_cur[c * ROW_CHUNK:(c + 1) * ROW_CHUNK, :]
                o_ref[sl, :] = (x_ref[sl, :]
                                + y * lax.rsqrt(jnp.mean(y * y, axis=-1, keepdims=True) + EPS) * gate_post)


def _mlp(h2, w_up, w_down, x1, mod, g_post, *, rows_per_group, tm=512, tf=1024):
    M, D = x1.shape
    Fh = w_up.shape[1]
    tm = min(tm, M)
    assert Fh // tf >= 2
    blocks = (_nbytes((tm, D), BF16) + 2 * _nbytes((D, tf), BF16) + 2 * _nbytes((tm, D), F32)
              + _nbytes((N_MOD + 1, D), F32))
    return pl.pallas_call(
        _mlp_kernel,
        out_shape=jax.ShapeDtypeStruct((M, D), F32),
        grid=(M // tm, Fh // tf),
        in_specs=[pl.BlockSpec((tm, D), lambda i, j: (i, 0)),
                  pl.BlockSpec((D, tf), lambda i, j: (0, j)),
                  pl.BlockSpec((tf, D), lambda i, j: (j, 0)),
                  pl.BlockSpec((tm, D), lambda i, j: (i, 0)),
                  pl.BlockSpec((None, N_MOD, D), _mod_map(mod.shape[0], rows_per_group, tm)),
                  pl.BlockSpec((1, D), lambda i, j: (0, 0))],
        out_specs=pl.BlockSpec((tm, D), lambda i, j: (i, 0)),
        compiler_params=pltpu.CompilerParams(
            dimension_semantics=("parallel", "arbitrary"),
            vmem_limit_bytes=_vmem_limit(blocks, temp_bytes=2 * _nbytes((tm, tf), F32))),
        name="mlp",
    )(h2, w_up, w_down, x1, mod, g_post)


def _rope_tables(T, rope_half):
    rows = T // GRID_W
    row = jnp.repeat(jnp.arange(rows, dtype=F32), GRID_W)
    col = jnp.tile(jnp.arange(GRID_W, dtype=F32), rows)
    inv = ROPE_BASE ** (-jnp.arange(0, rope_half, 2, dtype=F32) / rope_half)
    ar, ac = row[:, None] * inv, col[:, None] * inv
    cos = jnp.concatenate([jnp.cos(ar), jnp.cos(ac), jnp.cos(ar), jnp.cos(ac)], axis=-1)
    sin = jnp.concatenate([-jnp.sin(ar), -jnp.sin(ac), jnp.sin(ar), jnp.sin(ac)], axis=-1)
    return cos, sin


def kernel(x_prompt, x_sample, cache_k, cache_v, state_ret_fwd, state_ret_bwd, c, c_ctx, w_ada, b_ada, g_mix_pre, g_mix_post, g_mlp_pre, g_mlp_post, w_in, ret_gamma_logit_fwd, ret_gamma_logit_bwd, w_ret_o, lambda_q1, lambda_k1, lambda_q2, lambda_k2, g_diff_subln, w_diff_o, w_out, w_mlp_up, w_mlp_down):
    Bp, Tp, D = x_prompt.shape
    Bs, Ts, _ = x_sample.shape
    depth = w_in.shape[0]
    assert depth == 1
    l = 0
    H = N_RET_HEADS
    dk = D // H
    dv = 2 * dk
    dh = D // N_DIFF_HEADS // 2
    ret_qk_w, ret_v_w, diff_w = H * dk, H * dv, N_DIFF_HEADS * 2 * dh
    o_dq = 2 * ret_qk_w + 2 * ret_v_w
    o_dk, o_dv, o_gate = o_dq + diff_w, o_dq + 2 * diff_w, o_dq + 3 * diff_w
    in_w = o_gate + 2 * D
    assert w_in.shape[2] == in_w
    lam_init = 0.8 - 0.6 * math.exp(-0.3 * l)
    q_mult = float(dh) ** -0.5 * math.log2(math.e)

    w_in_l = w_in[l]
    g_mix_pre_l = g_mix_pre[l].reshape(1, D)
    g_mix_post_l = g_mix_post[l].reshape(1, D)
    g_mlp_pre_l = g_mlp_pre[l].reshape(1, D)
    g_mlp_post_l = g_mlp_post[l].reshape(1, D)
    g_subln_l = g_diff_subln[l].reshape(1, 2 * dh)
    lam_params = jnp.concatenate(
        [jnp.stack([lambda_q1[l], lambda_k1[l], lambda_q2[l], lambda_k2[l]]), jnp.zeros((4, dh), F32)], axis=0)
    lg = jnp.stack([ret_gamma_logit_fwd[l], ret_gamma_logit_bwd[l]], axis=1)
    lg = jnp.concatenate([lg, jnp.zeros((H, 6), F32)], axis=1)
    lg = jnp.broadcast_to(lg[:, :, None], (H, 8, V7X_LANES))

    n_cond = 16
    cond = jnp.concatenate([c, jnp.broadcast_to(c_ctx[None, :], (n_cond - Bs, D))], axis=0)
    mod = _adaln(cond, w_ada[l], b_ada[l].reshape(1, N_MOD * D)).reshape(n_cond, N_MOD, D)

    ret_offs = dict(off_q=0, off_k=ret_qk_w, off_v=2 * ret_qk_w, off_g=2 * ret_qk_w + ret_v_w)

    xs2 = x_sample.reshape(Bs * Ts, D)
    hs = _prenorm(xs2, mod, g_mix_pre_l, rows_per_group=Ts)
    sa = _proj(hs, w_in_l, (0, o_dq), out_dtype=BF16, tm=2048, name="proj_ret").reshape(Bs, Ts, -1)
    sqk, w_ret_o_b, w_diff_o_b, w_out_b = _proj(
        hs, w_in_l, (o_dq, o_dv), out_dtype=BF16, n_q_cols=diff_w, q_mult=q_mult,
        rope_tables=_rope_tables(Ts, dh // 2), cast_params=(w_ret_o[l], w_diff_o[l], w_out[l]),
        name="proj_qk_rope")
    sqk = sqk.reshape(Bs, Ts, 2 * diff_w)
    svg = _proj(hs, w_in_l, (o_dv, in_w), out_dtype=BF16, tm=2048, name="proj_vg")
    (ret_g,) = _retention(sa, lg, (state_ret_fwd, state_ret_bwd), emit_state=False, dk=dk, dv=dv, **ret_offs)
    caches = (cache_k[:, l].reshape(Bs, -1, diff_w), cache_v[:, l].reshape(Bs, -1, diff_w))
    diff_n = _diffattn((sqk, 0), (sqk, diff_w), (svg.reshape(Bs, Ts, -1), 0),
                       lam_params, g_subln_l, caches, lam_init=lam_init, dh=dh, heads_per_step=1)
    merged, w_up_b, w_down_b = _merge_a(
        ret_g.reshape(Bs * Ts, ret_v_w), diff_n.reshape(Bs * Ts, diff_w), w_ret_o_b, w_diff_o_b, svg,
        off_gr=diff_w, off_gd=diff_w + D, cast_params=(w_mlp_up[l], w_mlp_down[l]))

    def tail(merged, x2, B, T, rows_per_group):
        x1, h2 = _merge_b(merged, w_out_b, x2, mod, g_mix_post_l, g_mlp_pre_l, rows_per_group=rows_per_group)
        y = _mlp(h2, w_up_b, w_down_b, x1, mod, g_mlp_post_l, rows_per_group=rows_per_group)
        return y.reshape(B, T, D)

    y_sample = tail(merged, xs2, Bs, Ts, Ts)

    xp2 = x_prompt.reshape(Bp * Tp, D)
    hp = _prenorm(xp2, mod, g_mix_pre_l, rows_per_group=None)
    pa = _proj(hp, w_in_l, (0, o_dq), out_dtype=BF16, tm=2048, name="proj_ret").reshape(Bp, Tp, -1)
    pq = _proj(hp, w_in_l, (o_dq, o_dk), out_dtype=BF16, n_q_cols=diff_w, q_mult=q_mult,
               name="proj_q").reshape(Bp, Tp, diff_w)
    pk = _proj(hp, w_in_l, (o_dk, o_dv), out_dtype=F32, name="proj_k")
    pv = _proj(hp, w_in_l, (o_dv, o_gate), out_dtype=F32, name="proj_v")
    pg = _proj(hp, w_in_l, (o_gate, in_w), out_dtype=BF16, tm=2048, name="proj_gates")
    ret_g, new_state_fwd, new_state_bwd = _retention(pa, lg, None, emit_state=True, dk=dk, dv=dv,
                                                     batch_block=4, **ret_offs)
    diff_n = _diffattn((pq, 0), (pk.reshape(Bp, Tp, diff_w), 0), (pv.reshape(Bp, Tp, diff_w), 0),
                       lam_params, g_subln_l, None, lam_init=lam_init, dh=dh, heads_per_step=N_DIFF_HEADS)
    merged = _merge_a(ret_g.reshape(Bp * Tp, ret_v_w), diff_n.reshape(Bp * Tp, diff_w), w_ret_o_b, w_diff_o_b, pg,
                      off_gr=0, off_gd=D)
    y_prompt = tail(merged, xp2, Bp, Tp, None)
    new_cache_k = pk.reshape(Bp, 1, Tp, N_DIFF_HEADS, 2, dh)
    new_cache_v = pv.reshape(Bp, 1, Tp, N_DIFF_HEADS, 2 * dh)

    return (y_prompt, y_sample, new_cache_k, new_cache_v, new_state_fwd, new_state_bwd)
```
